```python
import math
import jax
import jax.numpy as jnp
from jax import lax
import numpy as np

D_MODEL = 1024
BATCH = 2
SEQ = 16384
DEPTH = 4
DEC_BATCH = 1
DEC_SEQ = 16384
PAST_LEN = 128

N_MIXERS = 3
EXPAND = 2
D_INNER = EXPAND * D_MODEL
EPS = 1e-6

SG_CHUNK = 128
SG_GROUPS = 8
SG_GDIM = D_INNER // SG_GROUPS

GLA_HEADS = 4
GLA_KEY = D_MODEL // 2
GLA_DK = GLA_KEY // GLA_HEADS
GLA_DV = D_INNER // GLA_HEADS
GLA_RANK = 16
GLA_TAU = 16.0
GLA_CHUNK = 64
GLA_IN = 2 * GLA_KEY + 2 * D_INNER + 2 * GLA_RANK
GLA_SPLITS = (GLA_KEY, 2 * GLA_KEY, 2 * GLA_KEY + D_INNER, 2 * GLA_KEY + 2 * D_INNER,
              2 * GLA_KEY + 2 * D_INNER + GLA_RANK)

DIFF_HEADS = 8
DIFF_DQK = 128
DIFF_DV = D_INNER // DIFF_HEADS
DIFF_QBLOCK = 128
DIFF_IN = 4 * D_INNER

N_LAYERS_A = (DEPTH + 2) // N_MIXERS
N_LAYERS_B = (DEPTH + 1) // N_MIXERS
N_LAYERS_C = DEPTH // N_MIXERS

kernel_name = 'hybrid_bidir_sgu_gla_diffattn_encoder'


def rms_norm(x, g):
    xf = x.astype(jnp.float32)
    y = xf * lax.rsqrt(jnp.mean(xf * xf, axis=-1, keepdims=True) + EPS)
    return (y * g.astype(jnp.float32)).astype(x.dtype)


def spatial_gating_mixer(h, w_in, v_g, w_s, b_s, w_out):
    b, s, _ = h.shape
    n = s // SG_CHUNK
    u, v, z = jnp.split(h @ w_in, 3, axis=-1)
    v = rms_norm(v, v_g).reshape(b, n, SG_CHUNK, SG_GROUPS, SG_GDIM)
    sv = jnp.einsum('gqp,bnpgc->bnqgc', w_s, v) + jnp.swapaxes(b_s, 0, 1)[:, :, None]
    y = u * sv.reshape(b, s, D_INNER)
    return (y * jax.nn.silu(z)) @ w_out


def gla_direction(q, k, v, log_a, strict):
    b, s, nh, dk = q.shape
    dv = v.shape[-1]
    c = GLA_CHUNK
    n = s // c
    qc, kc, vc, lc = (t.astype(jnp.float32).reshape(b, n, c, nh, t.shape[-1]) for t in (q, k, v, log_a))
    cum = jnp.cumsum(lc, axis=2)
    ref = cum[:, :, c // 2:c // 2 + 1]
    scores = jnp.einsum('bnthd,bnshd->bnhts', qc * jnp.exp(cum - ref), kc * jnp.exp(ref - cum))
    idx = jnp.arange(c)
    mask = (idx[:, None] > idx[None, :]) if strict else (idx[:, None] >= idx[None, :])
    scores = jnp.where(mask, scores, 0.0)
    o_intra = jnp.einsum('bnhts,bnshv->bnthv', scores, vc)
    last = cum[:, :, -1:]
    q_inter = qc * jnp.exp(cum)
    k_inter = kc * jnp.exp(last - cum)
    decay = jnp.exp(last[:, :, 0])

    def step(state, xs):
        qi, ki, vi, di = xs
        o = jnp.einsum('bthd,bhdv->bthv', qi, state)
        state = state * di[..., None] + jnp.einsum('bthd,bthv->bhdv', ki, vi)
        return state, o

    state0 = jnp.zeros((b, nh, dk, dv), jnp.float32)
    xs = tuple(jnp.moveaxis(t, 1, 0) for t in (q_inter, k_inter, vc, decay))
    _, o_inter = lax.scan(step, state0, xs)
    o = o_intra + jnp.moveaxis(o_inter, 0, 1)
    return o.reshape(b, s, nh, dv)


def gla_mixer(h, w_in, w_gate, gate_bias, o_g, w_out):
    b, s, _ = h.shape
    q, k, v, g, a_f, a_b = jnp.split(h @ w_in, GLA_SPLITS, axis=-1)
    q = q.reshape(b, s, GLA_HEADS, GLA_DK) * GLA_DK ** -0.5
    k = k.reshape(b, s, GLA_HEADS, GLA_DK)
    v = v.reshape(b, s, GLA_HEADS, GLA_DV)

    def log_decay(code, w, bias):
        pre = (code @ w + bias).astype(jnp.float32)
        return (jax.nn.log_sigmoid(pre) / GLA_TAU).reshape(b, s, GLA_HEADS, GLA_DK)

    la_f = log_decay(a_f, w_gate[0], gate_bias[0])
    la_b = log_decay(a_b, w_gate[1], gate_bias[1])
    flip = lambda t: jnp.flip(t, axis=1)
    o_f = gla_direction(q, k, v, la_f, strict=False)
    o_b = flip(gla_direction(flip(q), flip(k), flip(v), flip(la_b), strict=True))
    o = rms_norm(o_f + o_b, o_g).astype(h.dtype).reshape(b, s, D_INNER)
    return (o * jax.nn.silu(g)) @ w_out


def diff_attn_mixer(h, w_in, q_g, k_g, lam, o_g, w_out, lambda_init):
    b, s, _ = h.shape
    nb = s // DIFF_QBLOCK
    q, k, v, z = jnp.split(h @ w_in, 4, axis=-1)
    q = rms_norm(q.reshape(b, s, DIFF_HEADS, 2, DIFF_DQK), q_g)
    k = rms_norm(k.reshape(b, s, DIFF_HEADS, 2, DIFF_DQK), k_g)
    kf = k.astype(jnp.float32)
    vf = v.reshape(b, s, DIFF_HEADS, DIFF_DV).astype(jnp.float32)
    lf = lam.astype(jnp.float32)
    lam_full = jnp.exp(jnp.sum(lf[0] * lf[1])) - jnp.exp(jnp.sum(lf[2] * lf[3])) + lambda_init
    slopes = jnp.asarray(2.0 ** (-8.0 * np.arange(1, DIFF_HEADS + 1) / DIFF_HEADS), jnp.float32)
    pos_k = jnp.arange(s, dtype=jnp.float32)
    qb = (q.astype(jnp.float32) * DIFF_DQK ** -0.5).reshape(b, nb, DIFF_QBLOCK, DIFF_HEADS, 2, DIFF_DQK)
    qb = jnp.moveaxis(qb, 1, 0)
    starts = jnp.arange(nb, dtype=jnp.float32) * DIFF_QBLOCK

    def block(args):
        qi, start = args
        logits = jnp.einsum('bqhmd,bkhmd->bhmqk', qi, kf)
        pos_q = start + jnp.arange(DIFF_QBLOCK, dtype=jnp.float32)
        dist = jnp.abs(pos_q[:, None] - pos_k[None, :])
        logits = logits - slopes[None, :, None, None, None] * dist[None, None, None]
        p = jax.nn.softmax(logits, axis=-1)
        attn = p[:, :, 0] - lam_full * p[:, :, 1]
        return jnp.einsum('bhqk,bkhv->bqhv', attn, vf)

    o = lax.map(block, (qb, starts))
    o = jnp.moveaxis(o, 0, 1).reshape(b, s, DIFF_HEADS, DIFF_DV)
    o = (rms_norm(o, o_g) * (1.0 - lambda_init)).astype(h.dtype).reshape(b, s, D_INNER)
    return (o * jax.nn.silu(z)) @ w_out


def trunk(x, norm_g, a_w_in, a_v_g, a_w_s, a_b_s, a_w_out,
          b_w_in, b_w_gate, b_gate_bias, b_o_g, b_w_out,
          c_w_in, c_q_g, c_k_g, c_lam, c_o_g, c_w_out):
    for i in range(DEPTH):
        h = rms_norm(x, norm_g[i])
        kind, j = i % N_MIXERS, i // N_MIXERS
        if kind == 0:
            y = spatial_gating_mixer(h, a_w_in[j], a_v_g[j], a_w_s[j], a_b_s[j], a_w_out[j])
        elif kind == 1:
            y = gla_mixer(h, b_w_in[j], b_w_gate[j], b_gate_bias[j], b_o_g[j], b_w_out[j])
        else:
            lambda_init = 0.8 - 0.6 * math.exp(-0.3 * i)
            y = diff_attn_mixer(h, c_w_in[j], c_q_g[j], c_k_g[j], c_lam[j], c_o_g[j], c_w_out[j], lambda_init)
        x = x + y.astype(x.dtype)
    return x


def setup_inputs(seed: int = 0) -> dict:
    key = jax.random.key(seed)
    ks = jax.random.split(key, 20)
    f32 = jnp.float32

    def nrm(k, shape, scale):
        return jax.random.normal(k, shape, f32) * scale

    return {
        'x_prompt': nrm(ks[0], (BATCH, SEQ, D_MODEL), 1.0),
        'x_sample': nrm(ks[1], (DEC_BATCH, DEC_SEQ, D_MODEL), 1.0),
        'norm_g': 1.0 + nrm(ks[2], (DEPTH, D_MODEL), 0.02),
        'a_w_in': nrm(ks[3], (N_LAYERS_A, D_MODEL, 3 * D_INNER), D_MODEL ** -0.5),
        'a_v_g': 1.0 + nrm(ks[4], (N_LAYERS_A, D_INNER), 0.02),
        'a_w_s': nrm(ks[5], (N_LAYERS_A, SG_GROUPS, SG_CHUNK, SG_CHUNK), SG_CHUNK ** -0.5),
        'a_b_s': 1.0 + nrm(ks[6], (N_LAYERS_A, SG_GROUPS, SG_CHUNK), 0.1),
        'a_w_out': nrm(ks[7], (N_LAYERS_A, D_INNER, D_MODEL), D_INNER ** -0.5),
        'b_w_in': nrm(ks[8], (N_LAYERS_B, D_MODEL, GLA_IN), D_MODEL ** -0.5),
        'b_w_gate': nrm(ks[9], (N_LAYERS_B, 2, GLA_RANK, GLA_KEY), GLA_RANK ** -0.5),
        'b_gate_bias': nrm(ks[10], (N_LAYERS_B, 2, GLA_KEY), 0.1),
        'b_o_g': 1.0 + nrm(ks[11], (N_LAYERS_B, GLA_DV), 0.02),
        'b_w_out': nrm(ks[12], (N_LAYERS_B, D_INNER, D_MODEL), D_INNER ** -0.5),
        'c_w_in': nrm(ks[13], (N_LAYERS_C, D_MODEL, DIFF_IN), D_MODEL ** -0.5),
        'c_q_g': 1.0 + nrm(ks[14], (N_LAYERS_C, DIFF_DQK), 0.02),
        'c_k_g': 1.0 + nrm(ks[15], (N_LAYERS_C, DIFF_DQK), 0.02),
        'c_lam': nrm(ks[16], (N_LAYERS_C, 4, DIFF_DQK), 0.1),
        'c_o_g': 1.0 + nrm(ks[17], (N_LAYERS_C, DIFF_DV), 0.02),
        'c_w_out': nrm(ks[18], (N_LAYERS_C, D_INNER, D_MODEL), D_INNER ** -0.5),
    }


def reference(x_prompt, x_sample, norm_g, a_w_in, a_v_g, a_w_s, a_b_s, a_w_out,
              b_w_in, b_w_gate, b_gate_bias, b_o_g, b_w_out,
              c_w_in, c_q_g, c_k_g, c_lam, c_o_g, c_w_out):
    y_prompt = trunk(x_prompt, norm_g, a_w_in, a_v_g, a_w_s, a_b_s, a_w_out,
                     b_w_in, b_w_gate, b_gate_bias, b_o_g, b_w_out,
                     c_w_in, c_q_g, c_k_g, c_lam, c_o_g, c_w_out)
    y_sample = trunk(x_sample, norm_g, a_w_in, a_v_g, a_w_s, a_b_s, a_w_out,
                     b_w_in, b_w_gate, b_gate_bias, b_o_g, b_w_out,
                     c_w_in, c_q_g, c_k_g, c_lam, c_o_g, c_w_out)
    return (y_prompt, y_sample)
```

```python
import functools
import math

import jax
import jax.numpy as jnp
import numpy as np
from jax import lax
from jax.experimental import pallas as pl
from jax.experimental.pallas import tpu as pltpu

F32 = jnp.float32
BF16 = jnp.bfloat16

D_MODEL = 1024
SEQ = 16384
DEPTH = 4
D_INNER = 2048
EPS = 1e-6

SG_CHUNK = 128
SG_GROUPS = 8
SG_GDIM = D_INNER // SG_GROUPS

GLA_HEADS = 4
GLA_KEY = 512
GLA_DK = 128
GLA_DV = 512
GLA_RANK = 16
GLA_TAU = 16.0
GLA_CHUNK = 64

DIFF_HEADS = 8
DIFF_DQK = 128
DIFF_DV = 256

LANES = 128
VMEM_LIMIT = 56 * 1024 * 1024

TM = 512
GLA_ROWS = 256
TQ = 512


def _cparams(sem):
    return pltpu.CompilerParams(dimension_semantics=sem, vmem_limit_bytes=VMEM_LIMIT)


def _resident(shape):
    nd = len(shape)
    return pl.BlockSpec(shape, lambda *_: (0,) * nd, pipeline_mode=pl.Buffered(1))


def _rms(x, g):
    return x * lax.rsqrt(jnp.mean(x * x, axis=-1, keepdims=True) + EPS) * g


def _silu(z):
    return z * jax.nn.sigmoid(z)


def _dot(a, b):
    return jnp.dot(a, b, preferred_element_type=F32)


def _dot_nt(a, b):
    return lax.dot_general(a, b, (((1,), (1,)), ((), ())), preferred_element_type=F32)


def _dot_tn(a, b):
    return lax.dot_general(a, b, (((0,), (0,)), ((), ())), preferred_element_type=F32)


def _inproj_a_kernel(x_ref, g_ref, w_ref, vg_ref, o_ref):
    h = _rms(x_ref[...], g_ref[...]).astype(BF16)
    for blk in range(3):
        lo = blk * D_INNER
        y = _dot(h, w_ref[:, lo:lo + D_INNER])
        if blk == 1:
            y = _rms(y, vg_ref[...])
        o_ref[:, lo:lo + D_INNER] = y.astype(BF16)


def _inproj_a(x, g, w, vg):
    t = x.shape[0]
    n = 3 * D_INNER
    return pl.pallas_call(
        _inproj_a_kernel,
        grid=(t // TM,),
        in_specs=[pl.BlockSpec((TM, D_MODEL), lambda i: (i, 0)),
                  _resident((1, D_MODEL)), _resident((D_MODEL, n)), _resident((1, D_INNER))],
        out_specs=pl.BlockSpec((TM, n), lambda i: (i, 0)),
        out_shape=jax.ShapeDtypeStruct((t, n), BF16),
        compiler_params=_cparams(("parallel",)),
        name="inproj_a",
    )(x, g, w, vg)


def _log_sigmoid(x):
    return -(jnp.maximum(-x, 0.0) + jnp.log1p(jnp.exp(-jnp.abs(x))))


def _split3(x):
    hi = x.astype(BF16)
    r1 = x - hi.astype(F32)
    mid = r1.astype(BF16)
    lo = (r1 - mid.astype(F32)).astype(BF16)
    return hi, mid, lo


def _inproj_b_kernel(x_ref, g_ref, w_ref, wa_ref, wg_ref, gb_ref, tril_ref, triu_ref,
                     o_ref, cf_ref, cb_ref):
    h = _rms(x_ref[...], g_ref[...]).astype(BF16)
    for blk in range(2):
        lo = blk * D_INNER
        o_ref[:, lo:lo + D_INNER] = _dot(h, w_ref[:, lo:lo + D_INNER]).astype(BF16)
    lo = 2 * D_INNER
    qk = _dot(h, w_ref[:, lo:lo + 2 * GLA_KEY])
    o_ref[:, lo:lo + GLA_KEY] = (qk[:, :GLA_KEY] * GLA_DK ** -0.5).astype(BF16)
    o_ref[:, lo + GLA_KEY:lo + 2 * GLA_KEY] = qk[:, GLA_KEY:].astype(BF16)
    code = _dot(h, wa_ref[...]).astype(BF16)
    la = _log_sigmoid(_dot(code, wg_ref[...]) + gb_ref[...]) / GLA_TAU
    cf = jnp.zeros((TM, GLA_KEY), F32)
    cb = jnp.zeros((TM, GLA_KEY), F32)
    for part_f, part_b in zip(_split3(la[:, :GLA_KEY]), _split3(la[:, GLA_KEY:])):
        cf = cf + _dot(tril_ref[...], part_f)
        cb = cb + _dot(triu_ref[...], part_b)
    cf_ref[...] = cf
    cb_ref[...] = cb


def _inproj_b(x, g, w_main, w_a, w_gate, gate_bias, tril, triu):
    t = x.shape[0]
    n = w_main.shape[1]
    return pl.pallas_call(
        _inproj_b_kernel,
        grid=(t // TM,),
        in_specs=[pl.BlockSpec((TM, D_MODEL), lambda i: (i, 0)),
                  _resident((1, D_MODEL)), _resident((D_MODEL, n)), _resident((D_MODEL, LANES)),
                  _resident((LANES, 2 * GLA_KEY)), _resident((1, 2 * GLA_KEY)),
                  _resident((TM, TM)), _resident((TM, TM))],
        out_specs=[pl.BlockSpec((TM, n), lambda i: (i, 0)),
                   pl.BlockSpec((TM, GLA_KEY), lambda i: (i, 0)),
                   pl.BlockSpec((TM, GLA_KEY), lambda i: (i, 0))],
        out_shape=[jax.ShapeDtypeStruct((t, n), BF16),
                   jax.ShapeDtypeStruct((t, GLA_KEY), F32),
                   jax.ShapeDtypeStruct((t, GLA_KEY), F32)],
        compiler_params=_cparams(("parallel",)),
        name="inproj_b",
    )(x, g, w_main, w_a, w_gate, gate_bias, tril, triu)


def _inproj_c_kernel(x_ref, g_ref, w_ref, qg_ref, kg_ref, o_ref):
    h = _rms(x_ref[...], g_ref[...]).astype(BF16)
    half = D_INNER // 2
    for blk in range(8):
        lo = blk * half
        y = _dot(h, w_ref[:, lo:lo + half])
        if blk < 4:
            gain = qg_ref[...] if blk < 2 else kg_ref[...]
            scale = DIFF_DQK ** -0.5 if blk < 2 else 1.0
            for grp in range(half // DIFF_DQK):
                sl = slice(grp * DIFF_DQK, (grp + 1) * DIFF_DQK)
                o_ref[:, lo + sl.start:lo + sl.stop] = (_rms(y[:, sl], gain) * scale).astype(BF16)
        else:
            o_ref[:, lo:lo + half] = y.astype(BF16)


def _inproj_c(x, g, w, qg, kg):
    t = x.shape[0]
    n = 4 * D_INNER
    return pl.pallas_call(
        _inproj_c_kernel,
        grid=(t // TM,),
        in_specs=[pl.BlockSpec((TM, D_MODEL), lambda i: (i, 0)),
                  _resident((1, D_MODEL)), _resident((D_MODEL, n)),
                  _resident((1, DIFF_DQK)), _resident((1, DIFF_DQK))],
        out_specs=pl.BlockSpec((TM, n), lambda i: (i, 0)),
        out_shape=jax.ShapeDtypeStruct((t, n), BF16),
        compiler_params=_cparams(("parallel",)),
        name="inproj_c",
    )(x, g, w, qg, kg)


def _outproj_a_kernel(u_ref, v_ref, z_ref, ws_ref, bs_ref, w_ref, x_ref, o_ref, y_ref):
    for c in range(TM // SG_CHUNK):
        rows = slice(c * SG_CHUNK, (c + 1) * SG_CHUNK)
        for grp in range(SG_GROUPS):
            cols = slice(grp * SG_GDIM, (grp + 1) * SG_GDIM)
            sv = _dot(ws_ref[grp], v_ref[rows, cols]) + bs_ref[grp]
            y = u_ref[rows, cols].astype(F32) * sv * _silu(z_ref[rows, cols].astype(F32))
            y_ref[rows, cols] = y.astype(BF16)
    o_ref[...] = x_ref[...] + _dot(y_ref[...], w_ref[...])


def _outproj_a(uvz, ws, bs, w, x):
    t = x.shape[0]
    return pl.pallas_call(
        _outproj_a_kernel,
        grid=(t // TM,),
        in_specs=[pl.BlockSpec((TM, D_INNER), lambda i: (i, 0)),
                  pl.BlockSpec((TM, D_INNER), lambda i: (i, 1)),
                  pl.BlockSpec((TM, D_INNER), lambda i: (i, 2)),
                  _resident((SG_GROUPS, SG_CHUNK, SG_CHUNK)), _resident((SG_GROUPS, SG_CHUNK, 1)),
                  _resident((D_INNER, D_MODEL)),
                  pl.BlockSpec((TM, D_MODEL), lambda i: (i, 0))],
        out_specs=pl.BlockSpec((TM, D_MODEL), lambda i: (i, 0)),
        out_shape=jax.ShapeDtypeStruct((t, D_MODEL), F32),
        scratch_shapes=[pltpu.VMEM((TM, D_INNER), BF16)],
        compiler_params=_cparams(("parallel",)),
        name="outproj_a",
    )(uvz, uvz, uvz, ws, bs, w, x)


def _outproj_b_kernel(of_ref, ob_ref, g_ref, og_ref, w_ref, x_ref, o_ref, y_ref):
    for hd in range(GLA_HEADS):
        cols = slice(hd * GLA_DV, (hd + 1) * GLA_DV)
        o = of_ref[:, cols].astype(F32) + ob_ref[:, cols].astype(F32)
        y = _rms(o, og_ref[...]) * _silu(g_ref[:, cols].astype(F32))
        y_ref[:, cols] = y.astype(BF16)
    o_ref[...] = x_ref[...] + _dot(y_ref[...], w_ref[...])


def _outproj_b(o_f, o_b, main, og, w, x):
    t = x.shape[0]
    return pl.pallas_call(
        _outproj_b_kernel,
        grid=(t // TM,),
        in_specs=[pl.BlockSpec((TM, D_INNER), lambda i: (i, 0)),
                  pl.BlockSpec((TM, D_INNER), lambda i: (i, 0)),
                  pl.BlockSpec((TM, D_INNER), lambda i: (i, 1)),
                  _resident((1, GLA_DV)), _resident((D_INNER, D_MODEL)),
                  pl.BlockSpec((TM, D_MODEL), lambda i: (i, 0))],
        out_specs=pl.BlockSpec((TM, D_MODEL), lambda i: (i, 0)),
        out_shape=jax.ShapeDtypeStruct((t, D_MODEL), F32),
        scratch_shapes=[pltpu.VMEM((TM, D_INNER), BF16)],
        compiler_params=_cparams(("parallel",)),
        name="outproj_b",
    )(o_f, o_b, main, og, w, x)


def _outproj_c_kernel(a_ref, z_ref, w_ref, x_ref, o_ref):
    y = (a_ref[...].astype(F32) * _silu(z_ref[...].astype(F32))).astype(BF16)
    o_ref[...] = x_ref[...] + _dot(y, w_ref[...])


def _outproj_c(o, qkvz, w, x):
    t = x.shape[0]
    return pl.pallas_call(
        _outproj_c_kernel,
        grid=(t // TM,),
        in_specs=[pl.BlockSpec((TM, D_INNER), lambda i: (i, 0)),
                  pl.BlockSpec((TM, D_INNER), lambda i: (i, 3)),
                  _resident((D_INNER, D_MODEL)),
                  pl.BlockSpec((TM, D_MODEL), lambda i: (i, 0))],
        out_specs=pl.BlockSpec((TM, D_MODEL), lambda i: (i, 0)),
        out_shape=jax.ShapeDtypeStruct((t, D_MODEL), F32),
        compiler_params=_cparams(("parallel",)),
        name="outproj_c",
    )(o, qkvz, w, x)


def _gla_direction(q_ref, k_ref, c_ref, v_ref, o_ref, s_ref, reverse):
    n_chunks = GLA_ROWS // GLA_CHUNK
    i_ref, i_last = (GLA_CHUNK // 2 - 1, 0) if reverse else (GLA_CHUNK // 2, GLA_CHUNK - 1)
    cum_t = c_ref[...].T
    row = lax.broadcasted_iota(jnp.int32, (GLA_CHUNK, GLA_CHUNK), 0)
    col = lax.broadcasted_iota(jnp.int32, (GLA_CHUNK, GLA_CHUNK), 1)
    mask = (row < col) if reverse else (row >= col)
    order = range(n_chunks - 1, -1, -1) if reverse else range(n_chunks)
    for c in order:
        rows = slice(c * GLA_CHUNK, (c + 1) * GLA_CHUNK)
        q = q_ref[rows, :].astype(F32)
        k = k_ref[rows, :].astype(F32)
        v = v_ref[rows, :]
        cum = c_ref[rows, :]
        ref = cum[i_ref:i_ref + 1, :]
        last = cum[i_last:i_last + 1, :]
        scores = _dot_nt((q * jnp.exp(cum - ref)).astype(BF16), (k * jnp.exp(ref - cum)).astype(BF16))
        scores = jnp.where(mask, scores, 0.0).astype(BF16)
        state = s_ref[...]
        o = _dot(scores, v) + _dot((q * jnp.exp(cum)).astype(BF16), state.astype(BF16))
        o_ref[rows, :] = o.astype(o_ref.dtype)
        decay = jnp.exp(cum_t[:, c * GLA_CHUNK + i_last:c * GLA_CHUNK + i_last + 1])
        s_ref[...] = state * decay + _dot_tn((k * jnp.exp(last - cum)).astype(BF16), v)


def _gla_kernel(qf_ref, kf_ref, cf_ref, vf_ref, qb_ref, kb_ref, cb_ref, vb_ref,
                of_ref, ob_ref, sf_ref, sb_ref):
    @pl.when(pl.program_id(2) == 0)
    def _():
        sf_ref[...] = jnp.zeros_like(sf_ref)
        sb_ref[...] = jnp.zeros_like(sb_ref)

    _gla_direction(qf_ref, kf_ref, cf_ref, vf_ref, of_ref, sf_ref, reverse=False)
    _gla_direction(qb_ref, kb_ref, cb_ref, vb_ref, ob_ref, sb_ref, reverse=True)


def _gla(main3, cf3, cb3):
    nb, s, _ = main3.shape
    steps = s // GLA_ROWS
    q_col0 = 2 * D_INNER // GLA_DK
    k_col0 = q_col0 + GLA_HEADS

    def fwd(col0):
        return lambda b, h, i: (b, i, col0 + h)

    def bwd(col0):
        return lambda b, h, i: (b, steps - 1 - i, col0 + h)

    def specs(idx):
        return [pl.BlockSpec((None, GLA_ROWS, GLA_DK), idx(q_col0)),
                pl.BlockSpec((None, GLA_ROWS, GLA_DK), idx(k_col0)),
                pl.BlockSpec((None, GLA_ROWS, GLA_DK), idx(0)),
                pl.BlockSpec((None, GLA_ROWS, GLA_DV), idx(0))]

    out_sd = jax.ShapeDtypeStruct((nb, s, D_INNER), BF16)
    return pl.pallas_call(
        _gla_kernel,
        grid=(nb, GLA_HEADS, steps),
        in_specs=specs(fwd) + specs(bwd),
        out_specs=[pl.BlockSpec((None, GLA_ROWS, GLA_DV), fwd(0)),
                   pl.BlockSpec((None, GLA_ROWS, GLA_DV), bwd(0))],
        out_shape=[out_sd, out_sd],
        scratch_shapes=[pltpu.VMEM((GLA_DK, GLA_DV), F32), pltpu.VMEM((GLA_DK, GLA_DV), F32)],
        compiler_params=_cparams(("parallel", "parallel", "arbitrary")),
        name="gla_scan",
    )(main3, main3, cf3, main3, main3, main3, cb3, main3)


def _attn_kernel(slope_ref, q_ref, k_ref, v_ref, absd_ref, kx_ref, lam_ref, og_ref, o_ref,
                 ql_ref, qr_ref, m_ref, l_ref, acc_ref, *, lambda_init):
    hd = pl.program_id(1)
    qi = pl.program_id(2)
    n_kv = k_ref.shape[0] // TQ
    slope = slope_ref[hd]

    lane = lax.broadcasted_iota(jnp.int32, (TQ, LANES), 1)
    ext = jnp.where(lane < 2, slope, 0.0).astype(BF16)
    for m in range(2):
        qm = q_ref[:, m * DIFF_DQK:(m + 1) * DIFF_DQK]
        ql_ref[m, :, :DIFF_DQK] = qm
        ql_ref[m, :, DIFF_DQK:] = ext
        qr_ref[m, :, :DIFF_DQK] = qm
        qr_ref[m, :, DIFF_DQK:] = -ext
    m_ref[...] = jnp.full(m_ref.shape, -1e30, F32)
    l_ref[...] = jnp.zeros_like(l_ref)
    acc_ref[...] = jnp.zeros_like(acc_ref)
    ii = lax.broadcasted_iota(jnp.int32, (TQ, 1), 0).astype(F32)

    def update(m, s, r, v):
        m_old = m_ref[m]
        m_new = jnp.maximum(m_old, jnp.max(s, axis=1, keepdims=True) + r)
        alpha = jnp.exp(m_old - m_new)
        p = jnp.exp(s - (m_new - r))
        l_ref[m] = alpha * l_ref[m] + jnp.sum(p, axis=1, keepdims=True)
        acc_ref[m] = alpha * acc_ref[m] + _dot(p.astype(BF16), v)
        m_ref[m] = m_new

    def off_diag(j, qa_ref, r):
        start = pl.multiple_of(j * TQ, TQ)
        k = k_ref[pl.ds(start, TQ), :]
        v = v_ref[pl.ds(start, TQ), :]
        for m in range(2):
            ka = jnp.concatenate([k[:, m * DIFF_DQK:(m + 1) * DIFF_DQK], kx_ref[...]], axis=1)
            update(m, _dot_nt(qa_ref[m], ka), r, v)

    def left(j, carry):
        off_diag(j, ql_ref, -slope * (((qi - j) * TQ).astype(F32) + ii))
        return carry

    def right(j, carry):
        off_diag(j, qr_ref, slope * (ii - ((j - qi) * TQ).astype(F32)))
        return carry

    lax.fori_loop(0, qi, left, 0)

    start = pl.multiple_of(qi * TQ, TQ)
    k = k_ref[pl.ds(start, TQ), :]
    v = v_ref[pl.ds(start, TQ), :]
    bias = -slope * absd_ref[...]
    zero = jnp.zeros((TQ, 1), F32)
    for m in range(2):
        cols = slice(m * DIFF_DQK, (m + 1) * DIFF_DQK)
        update(m, _dot_nt(q_ref[:, cols], k[:, cols]) + bias, zero, v)

    lax.fori_loop(qi + 1, n_kv, right, 0)

    lam = lam_ref[...]
    lam_full = (jnp.exp(jnp.sum(lam[0:1] * lam[1:2], axis=1, keepdims=True))
                - jnp.exp(jnp.sum(lam[2:3] * lam[3:4], axis=1, keepdims=True)) + lambda_init)
    o = acc_ref[0] * (1.0 / l_ref[0]) - lam_full * (acc_ref[1] * (1.0 / l_ref[1]))
    o_ref[...] = (_rms(o, og_ref[...]) * (1.0 - lambda_init)).astype(o_ref.dtype)


def _attention(qkvz3, slopes, absd, kx, lam, og, lambda_init):
    nb, s, _ = qkvz3.shape
    k_col0 = D_INNER // DIFF_DV
    v_col0 = 2 * k_col0
    return pl.pallas_call(
        functools.partial(_attn_kernel, lambda_init=lambda_init),
        grid=(nb, DIFF_HEADS, s // TQ),
        in_specs=[pl.BlockSpec(memory_space=pltpu.SMEM),
                  pl.BlockSpec((None, TQ, DIFF_DV), lambda b, h, i: (b, i, h)),
                  pl.BlockSpec((None, s, DIFF_DV), lambda b, h, i: (b, 0, k_col0 + h)),
                  pl.BlockSpec((None, s, DIFF_DV), lambda b, h, i: (b, 0, v_col0 + h)),
                  _resident((TQ, TQ)), _resident((TQ, LANES)),
                  _resident((4, DIFF_DQK)), _resident((1, DIFF_DV))],
        out_specs=pl.BlockSpec((None, TQ, DIFF_DV), lambda b, h, i: (b, i, h)),
        out_shape=jax.ShapeDtypeStruct((nb, s, D_INNER), BF16),
        scratch_shapes=[pltpu.VMEM((2, TQ, 2 * DIFF_DQK), BF16), pltpu.VMEM((2, TQ, 2 * DIFF_DQK), BF16),
                        pltpu.VMEM((2, TQ, 1), F32), pltpu.VMEM((2, TQ, 1), F32),
                        pltpu.VMEM((2, TQ, DIFF_DV), F32)],
        compiler_params=_cparams(("parallel", "parallel", "arbitrary")),
        name="diff_attention",
    )(slopes, qkvz3, qkvz3, qkvz3, absd, kx, lam, og)


def _layer_a(x, g, w_in, v_g, w_s, b_s, w_out):
    uvz = _inproj_a(x, g[None], w_in.astype(BF16), v_g[None])
    return _outproj_a(uvz, w_s.astype(BF16), b_s[:, :, None], w_out.astype(BF16), x)


def _layer_b(x, g, w_in, w_gate, gate_bias, o_g, w_out):
    nb = x.shape[0] // SEQ
    c_q, c_k, c_v, c_g, c_a = 0, GLA_KEY, 2 * GLA_KEY, 2 * GLA_KEY + D_INNER, 2 * GLA_KEY + 2 * D_INNER
    w_main = jnp.concatenate([w_in[:, c_v:c_g], w_in[:, c_g:c_a], w_in[:, c_q:c_k], w_in[:, c_k:c_v]],
                             axis=1).astype(BF16)
    w_a = jnp.pad(w_in[:, c_a:], ((0, 0), (0, LANES - 2 * GLA_RANK))).astype(BF16)
    wg = jnp.zeros((LANES, 2 * GLA_KEY), F32)
    wg = wg.at[:GLA_RANK, :GLA_KEY].set(w_gate[0]).at[GLA_RANK:2 * GLA_RANK, GLA_KEY:].set(w_gate[1])
    rows = np.arange(TM)
    same_chunk = (rows[:, None] // GLA_CHUNK) == (rows[None, :] // GLA_CHUNK)
    tril = jnp.asarray(same_chunk & (rows[None, :] <= rows[:, None]), BF16)
    triu = jnp.asarray(same_chunk & (rows[None, :] >= rows[:, None]), BF16)
    main, cf, cb = _inproj_b(x, g[None], w_main, w_a, wg.astype(BF16), gate_bias.reshape(1, -1), tril, triu)
    o_f, o_b = _gla(main.reshape(nb, SEQ, -1), cf.reshape(nb, SEQ, -1), cb.reshape(nb, SEQ, -1))
    return _outproj_b(o_f.reshape(-1, D_INNER), o_b.reshape(-1, D_INNER), main, o_g[None],
                      w_out.astype(BF16), x)


def _layer_c(x, g, w_in, q_g, k_g, lam, o_g, w_out, lambda_init):
    nb = x.shape[0] // SEQ
    qkvz = _inproj_c(x, g[None], w_in.astype(BF16), q_g[None], k_g[None])
    slopes = jnp.asarray(2.0 ** (-8.0 * np.arange(1, DIFF_HEADS + 1) / DIFF_HEADS), F32)
    pos = np.arange(TQ)
    absd = jnp.asarray(np.abs(pos[:, None] - pos[None, :]), F32)
    kx = np.zeros((TQ, LANES), np.float32)
    kx[:, 0] = (pos // 16) * 16
    kx[:, 1] = pos % 16
    o = _attention(qkvz.reshape(nb, SEQ, -1), slopes, absd, jnp.asarray(kx, BF16), lam, o_g[None],
                   lambda_init)
    return _outproj_c(o.reshape(-1, D_INNER), qkvz, w_out.astype(BF16), x)


def kernel(x_prompt, x_sample, norm_g, a_w_in, a_v_g, a_w_s, a_b_s, a_w_out, b_w_in, b_w_gate, b_gate_bias,
           b_o_g, b_w_out, c_w_in, c_q_g, c_k_g, c_lam, c_o_g, c_w_out):
    n_prompt = x_prompt.shape[0] * x_prompt.shape[1]
    x = jnp.concatenate([x_prompt.reshape(-1, D_MODEL), x_sample.reshape(-1, D_MODEL)], axis=0)
    for i in range(DEPTH):
        kind, j = i % 3, i // 3
        if kind == 0:
            x = _layer_a(x, norm_g[i], a_w_in[j], a_v_g[j], a_w_s[j], a_b_s[j], a_w_out[j])
        elif kind == 1:
            x = _layer_b(x, norm_g[i], b_w_in[j], b_w_gate[j], b_gate_bias[j], b_o_g[j], b_w_out[j])
        else:
            lambda_init = 0.8 - 0.6 * math.exp(-0.3 * i)
            x = _layer_c(x, norm_g[i], c_w_in[j], c_q_g[j], c_k_g[j], c_lam[j], c_o_g[j], c_w_out[j],
                         lambda_init)
    return x[:n_prompt].reshape(x_prompt.shape), x[n_prompt:].reshape(x_sample.shape)
```

```python
import functools
import math

import jax
import jax.numpy as jnp
import numpy as np
from jax import lax
from jax.experimental import pallas as pl
from jax.experimental.pallas import tpu as pltpu

F32 = jnp.float32
BF16 = jnp.bfloat16

D_MODEL = 1024
SEQ = 16384
DEPTH = 4
D_INNER = 2048
EPS = 1e-6

SG_CHUNK = 128
SG_GROUPS = 8
SG_GDIM = D_INNER // SG_GROUPS

GLA_HEADS = 4
GLA_KEY = 512
GLA_DK = 128
GLA_DV = 512
GLA_RANK = 16
GLA_TAU = 16.0
GLA_CHUNK = 64

DIFF_HEADS = 8
DIFF_DQK = 128
DIFF_DV = 256

LANES = 128
VMEM_LIMIT = 56 * 1024 * 1024

TM = 512
GLA_ROWS = 256
TQ = 1024
TK = 512
TQ_ONLINE = 512
LOG2E = math.log2(math.e)
MAX_UNSHIFTED_LOG2 = 80.0


def _cparams(sem):
    return pltpu.CompilerParams(dimension_semantics=sem, vmem_limit_bytes=VMEM_LIMIT)


def _resident(shape):
    nd = len(shape)
    return pl.BlockSpec(shape, lambda *_: (0,) * nd, pipeline_mode=pl.Buffered(1))


def _rms(x, g):
    return x * lax.rsqrt(jnp.mean(x * x, axis=-1, keepdims=True) + EPS) * g


def _silu(z):
    return z * jax.nn.sigmoid(z)


def _dot(a, b):
    return jnp.dot(a, b, preferred_element_type=F32)


def _dot_nt(a, b):
    return lax.dot_general(a, b, (((1,), (1,)), ((), ())), preferred_element_type=F32)


def _dot_tn(a, b):
    return lax.dot_general(a, b, (((0,), (0,)), ((), ())), preferred_element_type=F32)


def _inproj_a_kernel(x_ref, g_ref, w_ref, vg_ref, o_ref):
    h = _rms(x_ref[...], g_ref[...]).astype(BF16)
    for blk in range(3):
        lo = blk * D_INNER
        y = _dot(h, w_ref[:, lo:lo + D_INNER])
        if blk == 1:
            y = _rms(y, vg_ref[...])
        o_ref[:, lo:lo + D_INNER] = y.astype(BF16)


def _inproj_a(x, g, w, vg):
    t = x.shape[0]
    n = 3 * D_INNER
    return pl.pallas_call(
        _inproj_a_kernel,
        grid=(t // TM,),
        in_specs=[pl.BlockSpec((TM, D_MODEL), lambda i: (i, 0)),
                  _resident((1, D_MODEL)), _resident((D_MODEL, n)), _resident((1, D_INNER))],
        out_specs=pl.BlockSpec((TM, n), lambda i: (i, 0)),
        out_shape=jax.ShapeDtypeStruct((t, n), BF16),
        compiler_params=_cparams(("parallel",)),
        name="inproj_a",
    )(x, g, w, vg)


def _log_sigmoid(x):
    return -(jnp.maximum(-x, 0.0) + jnp.log1p(jnp.exp(-jnp.abs(x))))


def _split3(x):
    hi = x.astype(BF16)
    r1 = x - hi.astype(F32)
    mid = r1.astype(BF16)
    lo = (r1 - mid.astype(F32)).astype(BF16)
    return hi, mid, lo


def _inproj_b_kernel(x_ref, g_ref, w_ref, wa_ref, wg_ref, gb_ref, tril_ref, triu_ref,
                     o_ref, cf_ref, cb_ref):
    h = _rms(x_ref[...], g_ref[...]).astype(BF16)
    for blk in range(2):
        lo = blk * D_INNER
        o_ref[:, lo:lo + D_INNER] = _dot(h, w_ref[:, lo:lo + D_INNER]).astype(BF16)
    lo = 2 * D_INNER
    qk = _dot(h, w_ref[:, lo:lo + 2 * GLA_KEY])
    o_ref[:, lo:lo + GLA_KEY] = (qk[:, :GLA_KEY] * GLA_DK ** -0.5).astype(BF16)
    o_ref[:, lo + GLA_KEY:lo + 2 * GLA_KEY] = qk[:, GLA_KEY:].astype(BF16)
    code = _dot(h, wa_ref[...]).astype(BF16)
    la = _log_sigmoid(_dot(code, wg_ref[...]) + gb_ref[...]) / GLA_TAU
    cf = jnp.zeros((TM, GLA_KEY), F32)
    cb = jnp.zeros((TM, GLA_KEY), F32)
    for part_f, part_b in zip(_split3(la[:, :GLA_KEY]), _split3(la[:, GLA_KEY:])):
        cf = cf + _dot(tril_ref[...], part_f)
        cb = cb + _dot(triu_ref[...], part_b)
    cf_ref[...] = cf
    cb_ref[...] = cb


def _inproj_b(x, g, w_main, w_a, w_gate, gate_bias, tril, triu):
    t = x.shape[0]
    n = w_main.shape[1]
    return pl.pallas_call(
        _inproj_b_kernel,
        grid=(t // TM,),
        in_specs=[pl.BlockSpec((TM, D_MODEL), lambda i: (i, 0)),
                  _resident((1, D_MODEL)), _resident((D_MODEL, n)), _resident((D_MODEL, LANES)),
                  _resident((LANES, 2 * GLA_KEY)), _resident((1, 2 * GLA_KEY)),
                  _resident((TM, TM)), _resident((TM, TM))],
        out_specs=[pl.BlockSpec((TM, n), lambda i: (i, 0)),
                   pl.BlockSpec((TM, GLA_KEY), lambda i: (i, 0)),
                   pl.BlockSpec((TM, GLA_KEY), lambda i: (i, 0))],
        out_shape=[jax.ShapeDtypeStruct((t, n), BF16),
                   jax.ShapeDtypeStruct((t, GLA_KEY), F32),
                   jax.ShapeDtypeStruct((t, GLA_KEY), F32)],
        compiler_params=_cparams(("parallel",)),
        name="inproj_b",
    )(x, g, w_main, w_a, w_gate, gate_bias, tril, triu)


def _inproj_c_kernel(x_ref, g_ref, w_ref, qg_ref, kg_ref, o_ref):
    h = _rms(x_ref[...], g_ref[...]).astype(BF16)
    half = D_INNER // 2
    for blk in range(8):
        lo = blk * half
        y = _dot(h, w_ref[:, lo:lo + half])
        if blk < 4:
            gain = qg_ref[...] if blk < 2 else kg_ref[...]
            scale = DIFF_DQK ** -0.5 * LOG2E if blk < 2 else 1.0
            for grp in range(half // DIFF_DQK):
                sl = slice(grp * DIFF_DQK, (grp + 1) * DIFF_DQK)
                o_ref[:, lo + sl.start:lo + sl.stop] = (_rms(y[:, sl], gain) * scale).astype(BF16)
        else:
            o_ref[:, lo:lo + half] = y.astype(BF16)


def _inproj_c(x, g, w, qg, kg):
    t = x.shape[0]
    n = 4 * D_INNER
    return pl.pallas_call(
        _inproj_c_kernel,
        grid=(t // TM,),
        in_specs=[pl.BlockSpec((TM, D_MODEL), lambda i: (i, 0)),
                  _resident((1, D_MODEL)), _resident((D_MODEL, n)),
                  _resident((1, DIFF_DQK)), _resident((1, DIFF_DQK))],
        out_specs=pl.BlockSpec((TM, n), lambda i: (i, 0)),
        out_shape=jax.ShapeDtypeStruct((t, n), BF16),
        compiler_params=_cparams(("parallel",)),
        name="inproj_c",
    )(x, g, w, qg, kg)


def _outproj_a_kernel(u_ref, v_ref, z_ref, ws_ref, bs_ref, w_ref, x_ref, o_ref, y_ref):
    for c in range(TM // SG_CHUNK):
        rows = slice(c * SG_CHUNK, (c + 1) * SG_CHUNK)
        for grp in range(SG_GROUPS):
            cols = slice(grp * SG_GDIM, (grp + 1) * SG_GDIM)
            sv = _dot(ws_ref[grp], v_ref[rows, cols]) + bs_ref[grp]
            y = u_ref[rows, cols].astype(F32) * sv * _silu(z_ref[rows, cols].astype(F32))
            y_ref[rows, cols] = y.astype(BF16)
    o_ref[...] = x_ref[...] + _dot(y_ref[...], w_ref[...])


def _outproj_a(uvz, ws, bs, w, x):
    t = x.shape[0]
    return pl.pallas_call(
        _outproj_a_kernel,
        grid=(t // TM,),
        in_specs=[pl.BlockSpec((TM, D_INNER), lambda i: (i, 0)),
                  pl.BlockSpec((TM, D_INNER), lambda i: (i, 1)),
                  pl.BlockSpec((TM, D_INNER), lambda i: (i, 2)),
                  _resident((SG_GROUPS, SG_CHUNK, SG_CHUNK)), _resident((SG_GROUPS, SG_CHUNK, 1)),
                  _resident((D_INNER, D_MODEL)),
                  pl.BlockSpec((TM, D_MODEL), lambda i: (i, 0))],
        out_specs=pl.BlockSpec((TM, D_MODEL), lambda i: (i, 0)),
        out_shape=jax.ShapeDtypeStruct((t, D_MODEL), F32),
        scratch_shapes=[pltpu.VMEM((TM, D_INNER), BF16)],
        compiler_params=_cparams(("parallel",)),
        name="outproj_a",
    )(uvz, uvz, uvz, ws, bs, w, x)


def _outproj_b_kernel(of_ref, ob_ref, g_ref, og_ref, w_ref, x_ref, o_ref, y_ref):
    for hd in range(GLA_HEADS):
        cols = slice(hd * GLA_DV, (hd + 1) * GLA_DV)
        o = of_ref[:, cols].astype(F32) + ob_ref[:, cols].astype(F32)
        y = _rms(o, og_ref[...]) * _silu(g_ref[:, cols].astype(F32))
        y_ref[:, cols] = y.astype(BF16)
    o_ref[...] = x_ref[...] + _dot(y_ref[...], w_ref[...])


def _outproj_b(o_f, o_b, main, og, w, x):
    t = x.shape[0]
    return pl.pallas_call(
        _outproj_b_kernel,
        grid=(t // TM,),
        in_specs=[pl.BlockSpec((TM, D_INNER), lambda i: (i, 0)),
                  pl.BlockSpec((TM, D_INNER), lambda i: (i, 0)),
                  pl.BlockSpec((TM, D_INNER), lambda i: (i, 1)),
                  _resident((1, GLA_DV)), _resident((D_INNER, D_MODEL)),
                  pl.BlockSpec((TM, D_MODEL), lambda i: (i, 0))],
        out_specs=pl.BlockSpec((TM, D_MODEL), lambda i: (i, 0)),
        out_shape=jax.ShapeDtypeStruct((t, D_MODEL), F32),
        scratch_shapes=[pltpu.VMEM((TM, D_INNER), BF16)],
        compiler_params=_cparams(("parallel",)),
        name="outproj_b",
    )(o_f, o_b, main, og, w, x)


def _outproj_c_kernel(a_ref, z_ref, w_ref, x_ref, o_ref):
    y = (a_ref[...].astype(F32) * _silu(z_ref[...].astype(F32))).astype(BF16)
    o_ref[...] = x_ref[...] + _dot(y, w_ref[...])


def _outproj_c(o, qkvz, w, x):
    t = x.shape[0]
    return pl.pallas_call(
        _outproj_c_kernel,
        grid=(t // TM,),
        in_specs=[pl.BlockSpec((TM, D_INNER), lambda i: (i, 0)),
                  pl.BlockSpec((TM, D_INNER), lambda i: (i, 3)),
                  _resident((D_INNER, D_MODEL)),
                  pl.BlockSpec((TM, D_MODEL), lambda i: (i, 0))],
        out_specs=pl.BlockSpec((TM, D_MODEL), lambda i: (i, 0)),
        out_shape=jax.ShapeDtypeStruct((t, D_MODEL), F32),
        compiler_params=_cparams(("parallel",)),
        name="outproj_c",
    )(o, qkvz, w, x)


def _gla_direction(q_ref, k_ref, c_ref, v_ref, o_ref, s_ref, reverse):
    n_chunks = GLA_ROWS // GLA_CHUNK
    i_ref, i_last = (GLA_CHUNK // 2 - 1, 0) if reverse else (GLA_CHUNK // 2, GLA_CHUNK - 1)
    cum_t = c_ref[...].T
    row = lax.broadcasted_iota(jnp.int32, (GLA_CHUNK, GLA_CHUNK), 0)
    col = lax.broadcasted_iota(jnp.int32, (GLA_CHUNK, GLA_CHUNK), 1)
    mask = (row < col) if reverse else (row >= col)
    order = range(n_chunks - 1, -1, -1) if reverse else range(n_chunks)
    for c in order:
        rows = slice(c * GLA_CHUNK, (c + 1) * GLA_CHUNK)
        q = q_ref[rows, :].astype(F32)
        k = k_ref[rows, :].astype(F32)
        v = v_ref[rows, :]
        cum = c_ref[rows, :]
        ref = cum[i_ref:i_ref + 1, :]
        last = cum[i_last:i_last + 1, :]
        scores = _dot_nt((q * jnp.exp(cum - ref)).astype(BF16), (k * jnp.exp(ref - cum)).astype(BF16))
        scores = jnp.where(mask, scores, 0.0).astype(BF16)
        state = s_ref[...]
        o = _dot(scores, v) + _dot((q * jnp.exp(cum)).astype(BF16), state.astype(BF16))
        o_ref[rows, :] = o.astype(o_ref.dtype)
        decay = jnp.exp(cum_t[:, c * GLA_CHUNK + i_last:c * GLA_CHUNK + i_last + 1])
        s_ref[...] = state * decay + _dot_tn((k * jnp.exp(last - cum)).astype(BF16), v)


def _gla_kernel(qf_ref, kf_ref, cf_ref, vf_ref, qb_ref, kb_ref, cb_ref, vb_ref,
                of_ref, ob_ref, sf_ref, sb_ref):
    @pl.when(pl.program_id(2) == 0)
    def _():
        sf_ref[...] = jnp.zeros_like(sf_ref)
        sb_ref[...] = jnp.zeros_like(sb_ref)

    _gla_direction(qf_ref, kf_ref, cf_ref, vf_ref, of_ref, sf_ref, reverse=False)
    _gla_direction(qb_ref, kb_ref, cb_ref, vb_ref, ob_ref, sb_ref, reverse=True)


def _gla(main3, cf3, cb3):
    nb, s, _ = main3.shape
    steps = s // GLA_ROWS
    q_col0 = 2 * D_INNER // GLA_DK
    k_col0 = q_col0 + GLA_HEADS

    def fwd(col0):
        return lambda b, h, i: (b, i, col0 + h)

    def bwd(col0):
        return lambda b, h, i: (b, steps - 1 - i, col0 + h)

    def specs(idx):
        return [pl.BlockSpec((None, GLA_ROWS, GLA_DK), idx(q_col0)),
                pl.BlockSpec((None, GLA_ROWS, GLA_DK), idx(k_col0)),
                pl.BlockSpec((None, GLA_ROWS, GLA_DK), idx(0)),
                pl.BlockSpec((None, GLA_ROWS, GLA_DV), idx(0))]

    out_sd = jax.ShapeDtypeStruct((nb, s, D_INNER), BF16)
    return pl.pallas_call(
        _gla_kernel,
        grid=(nb, GLA_HEADS, steps),
        in_specs=specs(fwd) + specs(bwd),
        out_specs=[pl.BlockSpec((None, GLA_ROWS, GLA_DV), fwd(0)),
                   pl.BlockSpec((None, GLA_ROWS, GLA_DV), bwd(0))],
        out_shape=[out_sd, out_sd],
        scratch_shapes=[pltpu.VMEM((GLA_DK, GLA_DV), F32), pltpu.VMEM((GLA_DK, GLA_DV), F32)],
        compiler_params=_cparams(("parallel", "parallel", "arbitrary")),
        name="gla_scan",
    )(main3, main3, cf3, main3, main3, main3, cb3, main3)


def _attn_finish(acc_ref, l0, l1, lam_ref, og_ref, o_ref, lambda_init):
    lam = lam_ref[...]
    lam_full = (jnp.exp(jnp.sum(lam[0:1] * lam[1:2], axis=1, keepdims=True))
                - jnp.exp(jnp.sum(lam[2:3] * lam[3:4], axis=1, keepdims=True)) + lambda_init)
    o = acc_ref[0] * (1.0 / l0) - lam_full * (acc_ref[1] * (1.0 / l1))
    o_ref[...] = (_rms(o, og_ref[...]) * (1.0 - lambda_init)).astype(o_ref.dtype)


def _lane_select(lane, pieces):
    out = 0.0
    for c, piece in reversed(list(enumerate(pieces))):
        out = jnp.where(lane == c, piece, out)
    return out


def _attn_bounded_kernel(sl_ref, q_ref, k_ref, v_ref, lam_ref, og_ref, o_ref, qa_ref, l_ref, acc_ref, *,
                         lambda_init):
    hd = pl.program_id(1)
    qi = pl.program_id(2)
    ratio = TQ // TK
    n_off = k_ref.shape[0] // TK - ratio
    sl = sl_ref[hd]

    lane_q = lax.broadcasted_iota(jnp.int32, (TQ, LANES), 1)
    ii = lax.broadcasted_iota(jnp.int32, (TQ, LANES), 0).astype(F32)
    row_l = [p.astype(F32) for p in _split3(sl * ii)]
    row_r = [p.astype(F32) for p in _split3(sl * (TQ - ii))]
    q_ext = _lane_select(lane_q, [-1.0, -1.0, -1.0] + row_l + row_r).astype(BF16)
    for m in range(2):
        qa_ref[m, :, :DIFF_DQK] = q_ref[:, m * DIFF_DQK:(m + 1) * DIFF_DQK]
        qa_ref[m, :, DIFF_DQK:] = q_ext
    l_ref[...] = jnp.zeros_like(l_ref)
    acc_ref[...] = jnp.zeros_like(acc_ref)

    lane_k = lax.broadcasted_iota(jnp.int32, (TK, LANES), 1)
    jj = lax.broadcasted_iota(jnp.int32, (TK, LANES), 0).astype(F32)

    def accumulate(m, s, v):
        p = jnp.exp2(s)
        part = p[:, :LANES]
        for c in range(1, TK // LANES):
            part = part + p[:, c * LANES:(c + 1) * LANES]
        l_ref[m] += part
        acc_ref[m] += _dot(p.astype(BF16), v)

    def off_diag(t, carry):
        is_left = t < ratio * qi
        j = jnp.where(is_left, t, t + ratio)
        gap = jnp.where(is_left, qi * TQ - (j + 1) * TK, j * TK - (qi + 1) * TQ).astype(F32)
        col = sl * (gap + jnp.where(is_left, TK - jj, jj))
        use_l = jnp.where(is_left, -1.0, 0.0)
        use_r = jnp.where(is_left, 0.0, -1.0)
        pieces = [p.astype(F32) for p in _split3(col)] + [use_l] * 3 + [use_r] * 3
        k_ext = _lane_select(lane_k, pieces).astype(BF16)
        start = pl.multiple_of(j * TK, TK)
        k = k_ref[pl.ds(start, TK), :]
        v = v_ref[pl.ds(start, TK), :]
        for m in range(2):
            ka = jnp.concatenate([k[:, m * DIFF_DQK:(m + 1) * DIFF_DQK], k_ext], axis=1)
            accumulate(m, _dot_nt(qa_ref[m], ka), v)
        return carry

    lax.fori_loop(0, n_off, off_diag, 0)

    i_loc = lax.broadcasted_iota(jnp.int32, (TQ, TK), 0)
    j_loc = lax.broadcasted_iota(jnp.int32, (TQ, TK), 1)
    for d in range(ratio):
        start = pl.multiple_of((ratio * qi + d) * TK, TK)
        k = k_ref[pl.ds(start, TK), :]
        v = v_ref[pl.ds(start, TK), :]
        bias = sl * jnp.abs(i_loc - j_loc - d * TK).astype(F32)
        for m in range(2):
            cols = slice(m * DIFF_DQK, (m + 1) * DIFF_DQK)
            accumulate(m, _dot_nt(q_ref[:, cols], k[:, cols]) - bias, v)

    _attn_finish(acc_ref, jnp.sum(l_ref[0], axis=1, keepdims=True), jnp.sum(l_ref[1], axis=1, keepdims=True),
                 lam_ref, og_ref, o_ref, lambda_init)


def _attention_bounded(qkvz3, sl, lam, og, lambda_init):
    nb, s, _ = qkvz3.shape
    k_col0 = D_INNER // DIFF_DV
    v_col0 = 2 * k_col0
    return pl.pallas_call(
        functools.partial(_attn_bounded_kernel, lambda_init=lambda_init),
        grid=(nb, DIFF_HEADS, s // TQ),
        in_specs=[pl.BlockSpec(memory_space=pltpu.SMEM),
                  pl.BlockSpec((None, TQ, DIFF_DV), lambda b, h, i: (b, i, h)),
                  pl.BlockSpec((None, s, DIFF_DV), lambda b, h, i: (b, 0, k_col0 + h)),
                  pl.BlockSpec((None, s, DIFF_DV), lambda b, h, i: (b, 0, v_col0 + h)),
                  _resident((4, DIFF_DQK)), _resident((1, DIFF_DV))],
        out_specs=pl.BlockSpec((None, TQ, DIFF_DV), lambda b, h, i: (b, i, h)),
        out_shape=jax.ShapeDtypeStruct((nb, s, D_INNER), BF16),
        scratch_shapes=[pltpu.VMEM((2, TQ, 2 * DIFF_DQK), BF16),
                        pltpu.VMEM((2, TQ, LANES), F32),
                        pltpu.VMEM((2, TQ, DIFF_DV), F32)],
        compiler_params=_cparams(("parallel", "parallel", "arbitrary")),
        name="diff_attention_bounded",
    )(sl, qkvz3, qkvz3, qkvz3, lam, og)


def _attn_online_kernel(sl_ref, q_ref, k_ref, v_ref, absd_ref, kx_ref, lam_ref, og_ref, o_ref,
                        ql_ref, qr_ref, m_ref, l_ref, acc_ref, *, lambda_init):
    TQ = TQ_ONLINE
    hd = pl.program_id(1)
    qi = pl.program_id(2)
    n_kv = k_ref.shape[0] // TQ
    slope = sl_ref[hd]

    lane = lax.broadcasted_iota(jnp.int32, (TQ, LANES), 1)
    pieces = [p.astype(F32) for p in _split3(jnp.full((TQ, LANES), slope, F32))]
    ext = _lane_select(lane, pieces + pieces).astype(BF16)
    for m in range(2):
        qm = q_ref[:, m * DIFF_DQK:(m + 1) * DIFF_DQK]
        ql_ref[m, :, :DIFF_DQK] = qm
        ql_ref[m, :, DIFF_DQK:] = ext
        qr_ref[m, :, :DIFF_DQK] = qm
        qr_ref[m, :, DIFF_DQK:] = -ext
    m_ref[...] = jnp.full(m_ref.shape, -1e30, F32)
    l_ref[...] = jnp.zeros_like(l_ref)
    acc_ref[...] = jnp.zeros_like(acc_ref)
    ii = lax.broadcasted_iota(jnp.int32, (TQ, 1), 0).astype(F32)

    def update(m, s, r, v):
        m_old = m_ref[m]
        m_new = jnp.maximum(m_old, jnp.max(s, axis=1, keepdims=True) + r)
        alpha = jnp.exp2(m_old - m_new)
        p = jnp.exp2(s - (m_new - r))
        l_ref[m] = alpha * l_ref[m] + jnp.sum(p, axis=1, keepdims=True)
        acc_ref[m] = alpha * acc_ref[m] + _dot(p.astype(BF16), v)
        m_ref[m] = m_new

    def off_diag(j, qa_ref, r):
        start = pl.multiple_of(j * TQ, TQ)
        k = k_ref[pl.ds(start, TQ), :]
        v = v_ref[pl.ds(start, TQ), :]
        for m in range(2):
            ka = jnp.concatenate([k[:, m * DIFF_DQK:(m + 1) * DIFF_DQK], kx_ref[...]], axis=1)
            update(m, _dot_nt(qa_ref[m], ka), r, v)

    def left(j, carry):
        off_diag(j, ql_ref, -slope * (((qi - j) * TQ).astype(F32) + ii))
        return carry

    def right(j, carry):
        off_diag(j, qr_ref, slope * (ii - ((j - qi) * TQ).astype(F32)))
        return carry

    lax.fori_loop(0, qi, left, 0)

    start = pl.multiple_of(qi * TQ, TQ)
    k = k_ref[pl.ds(start, TQ), :]
    v = v_ref[pl.ds(start, TQ), :]
    bias = -slope * absd_ref[...]
    zero = jnp.zeros((TQ, 1), F32)
    for m in range(2):
        cols = slice(m * DIFF_DQK, (m + 1) * DIFF_DQK)
        update(m, _dot_nt(q_ref[:, cols], k[:, cols]) + bias, zero, v)

    lax.fori_loop(qi + 1, n_kv, right, 0)

    _attn_finish(acc_ref, l_ref[0], l_ref[1], lam_ref, og_ref, o_ref, lambda_init)


def _attention_online(qkvz3, sl, lam, og, lambda_init):
    TQ = TQ_ONLINE
    nb, s, _ = qkvz3.shape
    k_col0 = D_INNER // DIFF_DV
    v_col0 = 2 * k_col0
    pos = np.arange(TQ)
    absd = jnp.asarray(np.abs(pos[:, None] - pos[None, :]), F32)
    kx_np = np.zeros((TQ, LANES), np.float32)
    kx_np[:, 0:3] = ((pos // 16) * 16)[:, None]
    kx_np[:, 3:6] = (pos % 16)[:, None]
    kx = jnp.asarray(kx_np, BF16)
    return pl.pallas_call(
        functools.partial(_attn_online_kernel, lambda_init=lambda_init),
        grid=(nb, DIFF_HEADS, s // TQ),
        in_specs=[pl.BlockSpec(memory_space=pltpu.SMEM),
                  pl.BlockSpec((None, TQ, DIFF_DV), lambda b, h, i: (b, i, h)),
                  pl.BlockSpec((None, s, DIFF_DV), lambda b, h, i: (b, 0, k_col0 + h)),
                  pl.BlockSpec((None, s, DIFF_DV), lambda b, h, i: (b, 0, v_col0 + h)),
                  _resident((TQ, TQ)), _resident((TQ, LANES)),
                  _resident((4, DIFF_DQK)), _resident((1, DIFF_DV))],
        out_specs=pl.BlockSpec((None, TQ, DIFF_DV), lambda b, h, i: (b, i, h)),
        out_shape=jax.ShapeDtypeStruct((nb, s, D_INNER), BF16),
        scratch_shapes=[pltpu.VMEM((2, TQ, 2 * DIFF_DQK), BF16), pltpu.VMEM((2, TQ, 2 * DIFF_DQK), BF16),
                        pltpu.VMEM((2, TQ, 1), F32), pltpu.VMEM((2, TQ, 1), F32),
                        pltpu.VMEM((2, TQ, DIFF_DV), F32)],
        compiler_params=_cparams(("parallel", "parallel", "arbitrary")),
        name="diff_attention_online",
    )(sl, qkvz3, qkvz3, qkvz3, absd, kx, lam, og)


def _layer_a(x, g, w_in, v_g, w_s, b_s, w_out):
    uvz = _inproj_a(x, g[None], w_in.astype(BF16), v_g[None])
    return _outproj_a(uvz, w_s.astype(BF16), b_s[:, :, None], w_out.astype(BF16), x)


def _layer_b(x, g, w_in, w_gate, gate_bias, o_g, w_out):
    nb = x.shape[0] // SEQ
    c_q, c_k, c_v, c_g, c_a = 0, GLA_KEY, 2 * GLA_KEY, 2 * GLA_KEY + D_INNER, 2 * GLA_KEY + 2 * D_INNER
    w_main = jnp.concatenate([w_in[:, c_v:c_g], w_in[:, c_g:c_a], w_in[:, c_q:c_k], w_in[:, c_k:c_v]],
                             axis=1).astype(BF16)
    w_a = jnp.pad(w_in[:, c_a:], ((0, 0), (0, LANES - 2 * GLA_RANK))).astype(BF16)
    wg = jnp.zeros((LANES, 2 * GLA_KEY), F32)
    wg = wg.at[:GLA_RANK, :GLA_KEY].set(w_gate[0]).at[GLA_RANK:2 * GLA_RANK, GLA_KEY:].set(w_gate[1])
    rows = np.arange(TM)
    same_chunk = (rows[:, None] // GLA_CHUNK) == (rows[None, :] // GLA_CHUNK)
    tril = jnp.asarray(same_chunk & (rows[None, :] <= rows[:, None]), BF16)
    triu = jnp.asarray(same_chunk & (rows[None, :] >= rows[:, None]), BF16)
    main, cf, cb = _inproj_b(x, g[None], w_main, w_a, wg.astype(BF16), gate_bias.reshape(1, -1), tril, triu)
    o_f, o_b = _gla(main.reshape(nb, SEQ, -1), cf.reshape(nb, SEQ, -1), cb.reshape(nb, SEQ, -1))
    return _outproj_b(o_f.reshape(-1, D_INNER), o_b.reshape(-1, D_INNER), main, o_g[None],
                      w_out.astype(BF16), x)


def _layer_c(x, g, w_in, q_g, k_g, lam, o_g, w_out, lambda_init):
    nb = x.shape[0] // SEQ
    qkvz = _inproj_c(x, g[None], w_in.astype(BF16), q_g[None], k_g[None])
    sl = jnp.asarray(2.0 ** (-8.0 * np.arange(1, DIFF_HEADS + 1) / DIFF_HEADS) * LOG2E, F32)
    qkvz3 = qkvz.reshape(nb, SEQ, -1)
    logit_bound = jnp.max(jnp.abs(q_g)) * jnp.max(jnp.abs(k_g)) * (DIFF_DQK ** 0.5 * LOG2E * 1.02)
    o = lax.cond(logit_bound <= MAX_UNSHIFTED_LOG2,
                 lambda: _attention_bounded(qkvz3, sl, lam, o_g[None], lambda_init),
                 lambda: _attention_online(qkvz3, sl, lam, o_g[None], lambda_init))
    return _outproj_c(o.reshape(-1, D_INNER), qkvz, w_out.astype(BF16), x)


def kernel(x_prompt, x_sample, norm_g, a_w_in, a_v_g, a_w_s, a_b_s, a_w_out, b_w_in, b_w_gate, b_gate_bias,
           b_o_g, b_w_out, c_w_in, c_q_g, c_k_g, c_lam, c_o_g, c_w_out):
    n_prompt = x_prompt.shape[0] * x_prompt.shape[1]
    x = jnp.concatenate([x_prompt.reshape(-1, D_MODEL), x_sample.reshape(-1, D_MODEL)], axis=0)
    for i in range(DEPTH):
        kind, j = i % 3, i // 3
        if kind == 0:
            x = _layer_a(x, norm_g[i], a_w_in[j], a_v_g[j], a_w_s[j], a_b_s[j], a_w_out[j])
        elif kind == 1:
            x = _layer_b(x, norm_g[i], b_w_in[j], b_w_gate[j], b_gate_bias[j], b_o_g[j], b_w_out[j])
        else:
            lambda_init = 0.8 - 0.6 * math.exp(-0.3 * i)
            x = _layer_c(x, norm_g[i], c_w_in[j], c_q_g[j], c_k_g[j], c_lam[j], c_o_g[j], c_w_out[j],
                         lambda_init)
    return x[:n_prompt].reshape(x_prompt.shape), x[n_prompt:].reshape(x_sample.shape)
```

```python
import functools
import math

import jax
import jax.numpy as jnp
import numpy as np
from jax import lax
from jax.experimental import pallas as pl
from jax.experimental.pallas import tpu as pltpu

F32 = jnp.float32
BF16 = jnp.bfloat16

D_MODEL = 1024
SEQ = 16384
DEPTH = 4
D_INNER = 2048
EPS = 1e-6

SG_CHUNK = 128
SG_GROUPS = 8
SG_GDIM = D_INNER // SG_GROUPS

GLA_HEADS = 4
GLA_KEY = 512
GLA_DK = 128
GLA_DV = 512
GLA_RANK = 16
GLA_TAU = 16.0
GLA_CHUNK = 64

DIFF_HEADS = 8
DIFF_DQK = 128
DIFF_DV = 256

LANES = 128
VMEM_LIMIT = 56 * 1024 * 1024

TM = 512
GLA_ROWS = 256
TQ = 1024
TK = 512
TQ_ONLINE = 512
LOG2E = math.log2(math.e)
MAX_UNSHIFTED_LOG2 = 80.0


def _cparams(sem):
    return pltpu.CompilerParams(dimension_semantics=sem, vmem_limit_bytes=VMEM_LIMIT)


def _resident(shape):
    nd = len(shape)
    return pl.BlockSpec(shape, lambda *_: (0,) * nd, pipeline_mode=pl.Buffered(1))


def _rms(x, g):
    return x * lax.rsqrt(jnp.mean(x * x, axis=-1, keepdims=True) + EPS) * g


def _silu(z):
    return z * jax.nn.sigmoid(z)


def _dot(a, b):
    return jnp.dot(a, b, preferred_element_type=F32)


def _dot_nt(a, b):
    return lax.dot_general(a, b, (((1,), (1,)), ((), ())), preferred_element_type=F32)


def _dot_tn(a, b):
    return lax.dot_general(a, b, (((0,), (0,)), ((), ())), preferred_element_type=F32)


def _inproj_a_kernel(x_ref, g_ref, w_ref, vg_ref, o_ref):
    h = _rms(x_ref[...], g_ref[...]).astype(BF16)
    for blk in range(3):
        lo = blk * D_INNER
        y = _dot(h, w_ref[:, lo:lo + D_INNER])
        if blk == 1:
            y = _rms(y, vg_ref[...])
        o_ref[:, lo:lo + D_INNER] = y.astype(BF16)


def _inproj_a(x, g, w, vg):
    t = x.shape[0]
    n = 3 * D_INNER
    return pl.pallas_call(
        _inproj_a_kernel,
        grid=(t // TM,),
        in_specs=[pl.BlockSpec((TM, D_MODEL), lambda i: (i, 0)),
                  _resident((1, D_MODEL)), _resident((D_MODEL, n)), _resident((1, D_INNER))],
        out_specs=pl.BlockSpec((TM, n), lambda i: (i, 0)),
        out_shape=jax.ShapeDtypeStruct((t, n), BF16),
        compiler_params=_cparams(("parallel",)),
        name="inproj_a",
    )(x, g, w, vg)


def _log_sigmoid(x):
    return -(jnp.maximum(-x, 0.0) + jnp.log1p(jnp.exp(-jnp.abs(x))))


def _split3(x):
    hi = x.astype(BF16)
    r1 = x - hi.astype(F32)
    mid = r1.astype(BF16)
    lo = (r1 - mid.astype(F32)).astype(BF16)
    return hi, mid, lo


def _inproj_b_kernel(x_ref, g_ref, w_ref, wa_ref, wg_ref, gb_ref, tril_ref, triu_ref,
                     o_ref, cf_ref, cb_ref):
    h = _rms(x_ref[...], g_ref[...]).astype(BF16)
    for blk in range(2):
        lo = blk * D_INNER
        o_ref[:, lo:lo + D_INNER] = _dot(h, w_ref[:, lo:lo + D_INNER]).astype(BF16)
    lo = 2 * D_INNER
    qk = _dot(h, w_ref[:, lo:lo + 2 * GLA_KEY])
    o_ref[:, lo:lo + GLA_KEY] = (qk[:, :GLA_KEY] * GLA_DK ** -0.5).astype(BF16)
    o_ref[:, lo + GLA_KEY:lo + 2 * GLA_KEY] = qk[:, GLA_KEY:].astype(BF16)
    code = _dot(h, wa_ref[...]).astype(BF16)
    la = _log_sigmoid(_dot(code, wg_ref[...]) + gb_ref[...]) / GLA_TAU
    cf = jnp.zeros((TM, GLA_KEY), F32)
    cb = jnp.zeros((TM, GLA_KEY), F32)
    for part_f, part_b in zip(_split3(la[:, :GLA_KEY])[:2], _split3(la[:, GLA_KEY:])[:2]):
        cf = cf + _dot(tril_ref[...], part_f)
        cb = cb + _dot(triu_ref[...], part_b)
    cf_ref[...] = cf
    cb_ref[...] = cb


def _inproj_b(x, g, w_main, w_a, w_gate, gate_bias, tril, triu):
    t = x.shape[0]
    n = w_main.shape[1]
    return pl.pallas_call(
        _inproj_b_kernel,
        grid=(t // TM,),
        in_specs=[pl.BlockSpec((TM, D_MODEL), lambda i: (i, 0)),
                  _resident((1, D_MODEL)), _resident((D_MODEL, n)), _resident((D_MODEL, LANES)),
                  _resident((LANES, 2 * GLA_KEY)), _resident((1, 2 * GLA_KEY)),
                  _resident((TM, TM)), _resident((TM, TM))],
        out_specs=[pl.BlockSpec((TM, n), lambda i: (i, 0)),
                   pl.BlockSpec((TM, GLA_KEY), lambda i: (i, 0)),
                   pl.BlockSpec((TM, GLA_KEY), lambda i: (i, 0))],
        out_shape=[jax.ShapeDtypeStruct((t, n), BF16),
                   jax.ShapeDtypeStruct((t, GLA_KEY), F32),
                   jax.ShapeDtypeStruct((t, GLA_KEY), F32)],
        compiler_params=_cparams(("parallel",)),
        name="inproj_b",
    )(x, g, w_main, w_a, w_gate, gate_bias, tril, triu)


def _inproj_c_kernel(x_ref, g_ref, w_ref, qg_ref, kg_ref, o_ref):
    h = _rms(x_ref[...], g_ref[...]).astype(BF16)
    half = D_INNER // 2
    for blk in range(8):
        lo = blk * half
        y = _dot(h, w_ref[:, lo:lo + half])
        if blk < 4:
            gain = qg_ref[...] if blk < 2 else kg_ref[...]
            scale = DIFF_DQK ** -0.5 * LOG2E if blk < 2 else 1.0
            for grp in range(half // DIFF_DQK):
                sl = slice(grp * DIFF_DQK, (grp + 1) * DIFF_DQK)
                o_ref[:, lo + sl.start:lo + sl.stop] = (_rms(y[:, sl], gain) * scale).astype(BF16)
        else:
            o_ref[:, lo:lo + half] = y.astype(BF16)


def _inproj_c(x, g, w, qg, kg):
    t = x.shape[0]
    n = 4 * D_INNER
    return pl.pallas_call(
        _inproj_c_kernel,
        grid=(t // TM,),
        in_specs=[pl.BlockSpec((TM, D_MODEL), lambda i: (i, 0)),
                  _resident((1, D_MODEL)), _resident((D_MODEL, n)),
                  _resident((1, DIFF_DQK)), _resident((1, DIFF_DQK))],
        out_specs=pl.BlockSpec((TM, n), lambda i: (i, 0)),
        out_shape=jax.ShapeDtypeStruct((t, n), BF16),
        compiler_params=_cparams(("parallel",)),
        name="inproj_c",
    )(x, g, w, qg, kg)


def _outproj_a_kernel(u_ref, v_ref, z_ref, ws_ref, bs_ref, w_ref, x_ref, o_ref, y_ref):
    for c in range(TM // SG_CHUNK):
        rows = slice(c * SG_CHUNK, (c + 1) * SG_CHUNK)
        for grp in range(SG_GROUPS):
            cols = slice(grp * SG_GDIM, (grp + 1) * SG_GDIM)
            sv = _dot(ws_ref[grp], v_ref[rows, cols]) + bs_ref[grp]
            y = u_ref[rows, cols].astype(F32) * sv * _silu(z_ref[rows, cols].astype(F32))
            y_ref[rows, cols] = y.astype(BF16)
    o_ref[...] = x_ref[...] + _dot(y_ref[...], w_ref[...])


def _outproj_a(uvz, ws, bs, w, x):
    t = x.shape[0]
    return pl.pallas_call(
        _outproj_a_kernel,
        grid=(t // TM,),
        in_specs=[pl.BlockSpec((TM, D_INNER), lambda i: (i, 0)),
                  pl.BlockSpec((TM, D_INNER), lambda i: (i, 1)),
                  pl.BlockSpec((TM, D_INNER), lambda i: (i, 2)),
                  _resident((SG_GROUPS, SG_CHUNK, SG_CHUNK)), _resident((SG_GROUPS, SG_CHUNK, 1)),
                  _resident((D_INNER, D_MODEL)),
                  pl.BlockSpec((TM, D_MODEL), lambda i: (i, 0))],
        out_specs=pl.BlockSpec((TM, D_MODEL), lambda i: (i, 0)),
        out_shape=jax.ShapeDtypeStruct((t, D_MODEL), F32),
        scratch_shapes=[pltpu.VMEM((TM, D_INNER), BF16)],
        compiler_params=_cparams(("parallel",)),
        name="outproj_a",
    )(uvz, uvz, uvz, ws, bs, w, x)


def _outproj_b_kernel(of_ref, ob_ref, g_ref, og_ref, w_ref, x_ref, o_ref, y_ref):
    for hd in range(GLA_HEADS):
        cols = slice(hd * GLA_DV, (hd + 1) * GLA_DV)
        o = of_ref[:, cols].astype(F32) + ob_ref[:, cols].astype(F32)
        y = _rms(o, og_ref[...]) * _silu(g_ref[:, cols].astype(F32))
        y_ref[:, cols] = y.astype(BF16)
    o_ref[...] = x_ref[...] + _dot(y_ref[...], w_ref[...])


def _outproj_b(o_f, o_b, main, og, w, x):
    t = x.shape[0]
    return pl.pallas_call(
        _outproj_b_kernel,
        grid=(t // TM,),
        in_specs=[pl.BlockSpec((TM, D_INNER), lambda i: (i, 0)),
                  pl.BlockSpec((TM, D_INNER), lambda i: (i, 0)),
                  pl.BlockSpec((TM, D_INNER), lambda i: (i, 1)),
                  _resident((1, GLA_DV)), _resident((D_INNER, D_MODEL)),
                  pl.BlockSpec((TM, D_MODEL), lambda i: (i, 0))],
        out_specs=pl.BlockSpec((TM, D_MODEL), lambda i: (i, 0)),
        out_shape=jax.ShapeDtypeStruct((t, D_MODEL), F32),
        scratch_shapes=[pltpu.VMEM((TM, D_INNER), BF16)],
        compiler_params=_cparams(("parallel",)),
        name="outproj_b",
    )(o_f, o_b, main, og, w, x)


def _outproj_c_kernel(a_ref, z_ref, w_ref, x_ref, o_ref):
    y = (a_ref[...].astype(F32) * _silu(z_ref[...].astype(F32))).astype(BF16)
    o_ref[...] = x_ref[...] + _dot(y, w_ref[...])


def _outproj_c(o, qkvz, w, x):
    t = x.shape[0]
    return pl.pallas_call(
        _outproj_c_kernel,
        grid=(t // TM,),
        in_specs=[pl.BlockSpec((TM, D_INNER), lambda i: (i, 0)),
                  pl.BlockSpec((TM, D_INNER), lambda i: (i, 3)),
                  _resident((D_INNER, D_MODEL)),
                  pl.BlockSpec((TM, D_MODEL), lambda i: (i, 0))],
        out_specs=pl.BlockSpec((TM, D_MODEL), lambda i: (i, 0)),
        out_shape=jax.ShapeDtypeStruct((t, D_MODEL), F32),
        compiler_params=_cparams(("parallel",)),
        name="outproj_c",
    )(o, qkvz, w, x)


def _gla_chunk(q_ref, k_ref, c_ref, ct_ref, v_ref, o_ref, s_ref, hd, c, reverse):
    i_ref, i_last = (GLA_CHUNK // 2 - 1, 0) if reverse else (GLA_CHUNK // 2, GLA_CHUNK - 1)
    row = lax.broadcasted_iota(jnp.int32, (GLA_CHUNK, GLA_CHUNK), 0)
    col = lax.broadcasted_iota(jnp.int32, (GLA_CHUNK, GLA_CHUNK), 1)
    mask = (row < col) if reverse else (row >= col)
    rows = slice(c * GLA_CHUNK, (c + 1) * GLA_CHUNK)
    kcols = slice(hd * GLA_DK, (hd + 1) * GLA_DK)
    vcols = slice(hd * GLA_DV, (hd + 1) * GLA_DV)
    q = q_ref[rows, kcols].astype(F32)
    k = k_ref[rows, kcols].astype(F32)
    v = v_ref[rows, vcols]
    cum = c_ref[rows, kcols]
    ref = cum[i_ref:i_ref + 1, :]
    last = cum[i_last:i_last + 1, :]
    scores = _dot_nt((q * jnp.exp(cum - ref)).astype(BF16), (k * jnp.exp(ref - cum)).astype(BF16))
    scores = jnp.where(mask, scores, 0.0).astype(BF16)
    state = s_ref[hd]
    o = _dot(scores, v) + _dot((q * jnp.exp(cum)).astype(BF16), state.astype(BF16))
    o_ref[rows, vcols] = o.astype(o_ref.dtype)
    decay = jnp.exp(ct_ref[kcols, c * GLA_CHUNK + i_last:c * GLA_CHUNK + i_last + 1])
    s_ref[hd] = state * decay + _dot_tn((k * jnp.exp(last - cum)).astype(BF16), v)


def _gla_kernel(qf_ref, kf_ref, cf_ref, vf_ref, qb_ref, kb_ref, cb_ref, vb_ref,
                of_ref, ob_ref, sf_ref, sb_ref, ctf_ref, ctb_ref):
    @pl.when(pl.program_id(1) == 0)
    def _():
        sf_ref[...] = jnp.zeros_like(sf_ref)
        sb_ref[...] = jnp.zeros_like(sb_ref)

    ctf_ref[...] = cf_ref[...].T
    ctb_ref[...] = cb_ref[...].T
    n_chunks = GLA_ROWS // GLA_CHUNK
    for c in range(n_chunks):
        for hd in range(GLA_HEADS):
            _gla_chunk(qf_ref, kf_ref, cf_ref, ctf_ref, vf_ref, of_ref, sf_ref, hd, c, reverse=False)
            _gla_chunk(qb_ref, kb_ref, cb_ref, ctb_ref, vb_ref, ob_ref, sb_ref, hd, n_chunks - 1 - c,
                       reverse=True)


def _gla(main3, cf3, cb3):
    nb, s, _ = main3.shape
    steps = s // GLA_ROWS
    q_col = 2 * D_INNER // GLA_KEY
    k_col = q_col + 1

    def fwd(col):
        return lambda b, i: (b, i, col)

    def bwd(col):
        return lambda b, i: (b, steps - 1 - i, col)

    def specs(idx):
        return [pl.BlockSpec((None, GLA_ROWS, GLA_KEY), idx(q_col)),
                pl.BlockSpec((None, GLA_ROWS, GLA_KEY), idx(k_col)),
                pl.BlockSpec((None, GLA_ROWS, GLA_KEY), idx(0)),
                pl.BlockSpec((None, GLA_ROWS, D_INNER), idx(0))]

    out_sd = jax.ShapeDtypeStruct((nb, s, D_INNER), BF16)
    state = pltpu.VMEM((GLA_HEADS, GLA_DK, GLA_DV), F32)
    cum_t = pltpu.VMEM((GLA_KEY, GLA_ROWS), F32)
    return pl.pallas_call(
        _gla_kernel,
        grid=(nb, steps),
        in_specs=specs(fwd) + specs(bwd),
        out_specs=[pl.BlockSpec((None, GLA_ROWS, D_INNER), fwd(0)),
                   pl.BlockSpec((None, GLA_ROWS, D_INNER), bwd(0))],
        out_shape=[out_sd, out_sd],
        scratch_shapes=[state, state, cum_t, cum_t],
        compiler_params=_cparams(("parallel", "arbitrary")),
        name="gla_scan",
    )(main3, main3, cf3, main3, main3, main3, cb3, main3)


def _attn_finish(acc_ref, l0, l1, lam_ref, og_ref, o_ref, lambda_init):
    lam = lam_ref[...]
    lam_full = (jnp.exp(jnp.sum(lam[0:1] * lam[1:2], axis=1, keepdims=True))
                - jnp.exp(jnp.sum(lam[2:3] * lam[3:4], axis=1, keepdims=True)) + lambda_init)
    o = acc_ref[0] * (1.0 / l0) - lam_full * (acc_ref[1] * (1.0 / l1))
    o_ref[...] = (_rms(o, og_ref[...]) * (1.0 - lambda_init)).astype(o_ref.dtype)


def _lane_select(lane, pieces):
    out = 0.0
    for c, piece in reversed(list(enumerate(pieces))):
        out = jnp.where(lane == c, piece, out)
    return out


def _attn_bounded_kernel(sl_ref, q_ref, k_ref, v_ref, qx_ref, kx_ref, lam_ref, og_ref, o_ref,
                         qa_ref, p_ref, l_ref, acc_ref, *, lambda_init):
    hd = pl.program_id(1)
    qi = pl.program_id(2)
    ratio = TQ // TK
    n_off = k_ref.shape[0] // TK - ratio
    assert ratio == 2 and n_off >= 2 and n_off % 2 == 0
    sl = sl_ref[hd]

    for m in range(2):
        qa_ref[m, :, :DIFF_DQK] = q_ref[:, m * DIFF_DQK:(m + 1) * DIFF_DQK]
        qa_ref[m, :, DIFF_DQK:] = qx_ref[...]
    l_ref[...] = jnp.zeros_like(l_ref)
    acc_ref[...] = jnp.zeros_like(acc_ref)
    lane8 = lax.broadcasted_iota(jnp.int32, (8, LANES), 1)

    def tile_of(t):
        is_left = t < ratio * qi
        return is_left, jnp.where(is_left, t, t + ratio)

    def probabilities(slot, logits_of_map):
        for m in range(2):
            p = jnp.exp2(logits_of_map(m))
            part = p[:, :LANES]
            for c in range(1, TK // LANES):
                part = part + p[:, c * LANES:(c + 1) * LANES]
            l_ref[m] += part
            p_ref[slot, m] = p.astype(BF16)

    def off_diag(t, slot):
        is_left, j = tile_of(t)
        gap = jnp.where(is_left, qi * TQ - (j + 1) * TK, j * TK - (qi + 1) * TQ).astype(F32)
        gap_pieces = [p.astype(F32) for p in _split3(jnp.full((8, LANES), sl * gap, F32))]
        gap_row = _lane_select(lane8, [0.0] * 9 + gap_pieces)[0:1, :].astype(BF16)
        k_ext = kx_ref[jnp.where(is_left, 0, 1)] + gap_row
        k = k_ref[pl.ds(pl.multiple_of(j * TK, TK), TK), :]

        def logits(m):
            ka = jnp.concatenate([k[:, m * DIFF_DQK:(m + 1) * DIFF_DQK], k_ext], axis=1)
            return _dot_nt(qa_ref[m], ka)

        probabilities(slot, logits)

    def diag(d, slot):
        k = k_ref[pl.ds(pl.multiple_of((ratio * qi + d) * TK, TK), TK), :]
        i_loc = lax.broadcasted_iota(jnp.int32, (TQ, TK), 0)
        j_loc = lax.broadcasted_iota(jnp.int32, (TQ, TK), 1)
        bias = sl * jnp.abs(i_loc - j_loc - d * TK).astype(F32)

        def logits(m):
            cols = slice(m * DIFF_DQK, (m + 1) * DIFF_DQK)
            return _dot_nt(q_ref[:, cols], k[:, cols]) - bias

        probabilities(slot, logits)

    def weighted_values(j, slot):
        v = v_ref[pl.ds(pl.multiple_of(j * TK, TK), TK), :]
        for m in range(2):
            acc_ref[m] += _dot(p_ref[slot, m], v)

    off_diag(0, 0)

    def pair(u, carry):
        t = 2 * u
        off_diag(t + 1, 1)
        weighted_values(tile_of(t)[1], 0)
        off_diag(t + 2, 0)
        weighted_values(tile_of(t + 1)[1], 1)
        return carry

    lax.fori_loop(0, (n_off - 2) // 2, pair, 0)
    off_diag(n_off - 1, 1)
    weighted_values(tile_of(n_off - 2)[1], 0)
    diag(0, 0)
    weighted_values(tile_of(n_off - 1)[1], 1)
    diag(1, 1)
    weighted_values(ratio * qi, 0)
    weighted_values(ratio * qi + 1, 1)

    _attn_finish(acc_ref, jnp.sum(l_ref[0], axis=1, keepdims=True), jnp.sum(l_ref[1], axis=1, keepdims=True),
                 lam_ref, og_ref, o_ref, lambda_init)


def _split3_np(x):
    x = np.asarray(x, np.float32)
    hi = x.astype(BF16).astype(np.float32)
    mid = (x - hi).astype(BF16).astype(np.float32)
    lo = (x - hi - mid).astype(BF16).astype(np.float32)
    return [hi, mid, lo]


def _alibi_tables(sl_np):
    nh = len(sl_np)
    ii = np.arange(TQ, dtype=np.float32)
    jj = np.arange(TK, dtype=np.float32)
    qx = np.zeros((nh, TQ, LANES), np.float32)
    kx = np.zeros((nh, 2, TK, LANES), np.float32)
    for h, s in enumerate(np.asarray(sl_np, np.float32)):
        qx[h, :, 0:3] = -1.0
        qx[h, :, 9:12] = -1.0
        for c, piece in enumerate(_split3_np(s * ii)):
            qx[h, :, 3 + c] = piece
        for c, piece in enumerate(_split3_np(s * (TQ - ii))):
            qx[h, :, 6 + c] = piece
        for c, piece in enumerate(_split3_np(s * (TK - jj))):
            kx[h, 0, :, c] = piece
        for c, piece in enumerate(_split3_np(s * jj)):
            kx[h, 1, :, c] = piece
        kx[h, 0, :, 3:6] = -1.0
        kx[h, 1, :, 6:9] = -1.0
    return jnp.asarray(qx, BF16), jnp.asarray(kx, BF16)


def _attention_bounded(qkvz3, sl_np, lam, og, lambda_init):
    nb, s, _ = qkvz3.shape
    k_col0 = D_INNER // DIFF_DV
    v_col0 = 2 * k_col0
    qx, kx = _alibi_tables(sl_np)
    return pl.pallas_call(
        functools.partial(_attn_bounded_kernel, lambda_init=lambda_init),
        grid=(nb, DIFF_HEADS, s // TQ),
        in_specs=[pl.BlockSpec(memory_space=pltpu.SMEM),
                  pl.BlockSpec((None, TQ, DIFF_DV), lambda b, h, i: (b, i, h)),
                  pl.BlockSpec((None, s, DIFF_DV), lambda b, h, i: (b, 0, k_col0 + h)),
                  pl.BlockSpec((None, s, DIFF_DV), lambda b, h, i: (b, 0, v_col0 + h)),
                  pl.BlockSpec((None, TQ, LANES), lambda b, h, i: (h, 0, 0)),
                  pl.BlockSpec((None, 2, TK, LANES), lambda b, h, i: (h, 0, 0, 0)),
                  _resident((4, DIFF_DQK)), _resident((1, DIFF_DV))],
        out_specs=pl.BlockSpec((None, TQ, DIFF_DV), lambda b, h, i: (b, i, h)),
        out_shape=jax.ShapeDtypeStruct((nb, s, D_INNER), BF16),
        scratch_shapes=[pltpu.VMEM((2, TQ, 2 * DIFF_DQK), BF16),
                        pltpu.VMEM((2, 2, TQ, TK), BF16),
                        pltpu.VMEM((2, TQ, LANES), F32),
                        pltpu.VMEM((2, TQ, DIFF_DV), F32)],
        compiler_params=_cparams(("parallel", "parallel", "arbitrary")),
        name="diff_attention_bounded",
    )(jnp.asarray(sl_np, F32), qkvz3, qkvz3, qkvz3, qx, kx, lam, og)


def _attn_online_kernel(sl_ref, q_ref, k_ref, v_ref, absd_ref, kx_ref, lam_ref, og_ref, o_ref,
                        ql_ref, qr_ref, m_ref, l_ref, acc_ref, *, lambda_init):
    TQ = TQ_ONLINE
    hd = pl.program_id(1)
    qi = pl.program_id(2)
    n_kv = k_ref.shape[0] // TQ
    slope = sl_ref[hd]

    lane = lax.broadcasted_iota(jnp.int32, (TQ, LANES), 1)
    pieces = [p.astype(F32) for p in _split3(jnp.full((TQ, LANES), slope, F32))]
    ext = _lane_select(lane, pieces + pieces).astype(BF16)
    for m in range(2):
        qm = q_ref[:, m * DIFF_DQK:(m + 1) * DIFF_DQK]
        ql_ref[m, :, :DIFF_DQK] = qm
        ql_ref[m, :, DIFF_DQK:] = ext
        qr_ref[m, :, :DIFF_DQK] = qm
        qr_ref[m, :, DIFF_DQK:] = -ext
    m_ref[...] = jnp.full(m_ref.shape, -1e30, F32)
    l_ref[...] = jnp.zeros_like(l_ref)
    acc_ref[...] = jnp.zeros_like(acc_ref)
    ii = lax.broadcasted_iota(jnp.int32, (TQ, 1), 0).astype(F32)

    def update(m, s, r, v):
        m_old = m_ref[m]
        m_new = jnp.maximum(m_old, jnp.max(s, axis=1, keepdims=True) + r)
        alpha = jnp.exp2(m_old - m_new)
        p = jnp.exp2(s - (m_new - r))
        l_ref[m] = alpha * l_ref[m] + jnp.sum(p, axis=1, keepdims=True)
        acc_ref[m] = alpha * acc_ref[m] + _dot(p.astype(BF16), v)
        m_ref[m] = m_new

    def off_diag(j, qa_ref, r):
        start = pl.multiple_of(j * TQ, TQ)
        k = k_ref[pl.ds(start, TQ), :]
        v = v_ref[pl.ds(start, TQ), :]
        for m in range(2):
            ka = jnp.concatenate([k[:, m * DIFF_DQK:(m + 1) * DIFF_DQK], kx_ref[...]], axis=1)
            update(m, _dot_nt(qa_ref[m], ka), r, v)

    def left(j, carry):
        off_diag(j, ql_ref, -slope * (((qi - j) * TQ).astype(F32) + ii))
        return carry

    def right(j, carry):
        off_diag(j, qr_ref, slope * (ii - ((j - qi) * TQ).astype(F32)))
        return carry

    lax.fori_loop(0, qi, left, 0)

    start = pl.multiple_of(qi * TQ, TQ)
    k = k_ref[pl.ds(start, TQ), :]
    v = v_ref[pl.ds(start, TQ), :]
    bias = -slope * absd_ref[...]
    zero = jnp.zeros((TQ, 1), F32)
    for m in range(2):
        cols = slice(m * DIFF_DQK, (m + 1) * DIFF_DQK)
        update(m, _dot_nt(q_ref[:, cols], k[:, cols]) + bias, zero, v)

    lax.fori_loop(qi + 1, n_kv, right, 0)

    _attn_finish(acc_ref, l_ref[0], l_ref[1], lam_ref, og_ref, o_ref, lambda_init)


def _attention_online(qkvz3, sl, lam, og, lambda_init):
    TQ = TQ_ONLINE
    nb, s, _ = qkvz3.shape
    k_col0 = D_INNER // DIFF_DV
    v_col0 = 2 * k_col0
    pos = np.arange(TQ)
    absd = jnp.asarray(np.abs(pos[:, None] - pos[None, :]), F32)
    kx_np = np.zeros((TQ, LANES), np.float32)
    kx_np[:, 0:3] = ((pos // 16) * 16)[:, None]
    kx_np[:, 3:6] = (pos % 16)[:, None]
    kx = jnp.asarray(kx_np, BF16)
    return pl.pallas_call(
        functools.partial(_attn_online_kernel, lambda_init=lambda_init),
        grid=(nb, DIFF_HEADS, s // TQ),
        in_specs=[pl.BlockSpec(memory_space=pltpu.SMEM),
                  pl.BlockSpec((None, TQ, DIFF_DV), lambda b, h, i: (b, i, h)),
                  pl.BlockSpec((None, s, DIFF_DV), lambda b, h, i: (b, 0, k_col0 + h)),
                  pl.BlockSpec((None, s, DIFF_DV), lambda b, h, i: (b, 0, v_col0 + h)),
                  _resident((TQ, TQ)), _resident((TQ, LANES)),
                  _resident((4, DIFF_DQK)), _resident((1, DIFF_DV))],
        out_specs=pl.BlockSpec((None, TQ, DIFF_DV), lambda b, h, i: (b, i, h)),
        out_shape=jax.ShapeDtypeStruct((nb, s, D_INNER), BF16),
        scratch_shapes=[pltpu.VMEM((2, TQ, 2 * DIFF_DQK), BF16), pltpu.VMEM((2, TQ, 2 * DIFF_DQK), BF16),
                        pltpu.VMEM((2, TQ, 1), F32), pltpu.VMEM((2, TQ, 1), F32),
                        pltpu.VMEM((2, TQ, DIFF_DV), F32)],
        compiler_params=_cparams(("parallel", "parallel", "arbitrary")),
        name="diff_attention_online",
    )(sl, qkvz3, qkvz3, qkvz3, absd, kx, lam, og)


def _layer_a(x, g, w_in, v_g, w_s, b_s, w_out):
    uvz = _inproj_a(x, g[None], w_in.astype(BF16), v_g[None])
    return _outproj_a(uvz, w_s.astype(BF16), b_s[:, :, None], w_out.astype(BF16), x)


def _layer_b(x, g, w_in, w_gate, gate_bias, o_g, w_out):
    nb = x.shape[0] // SEQ
    c_q, c_k, c_v, c_g, c_a = 0, GLA_KEY, 2 * GLA_KEY, 2 * GLA_KEY + D_INNER, 2 * GLA_KEY + 2 * D_INNER
    w_main = jnp.concatenate([w_in[:, c_v:c_g], w_in[:, c_g:c_a], w_in[:, c_q:c_k], w_in[:, c_k:c_v]],
                             axis=1).astype(BF16)
    w_a = jnp.pad(w_in[:, c_a:], ((0, 0), (0, LANES - 2 * GLA_RANK))).astype(BF16)
    wg = jnp.zeros((LANES, 2 * GLA_KEY), F32)
    wg = wg.at[:GLA_RANK, :GLA_KEY].set(w_gate[0]).at[GLA_RANK:2 * GLA_RANK, GLA_KEY:].set(w_gate[1])
    rows = np.arange(TM)
    same_chunk = (rows[:, None] // GLA_CHUNK) == (rows[None, :] // GLA_CHUNK)
    tril = jnp.asarray(same_chunk & (rows[None, :] <= rows[:, None]), BF16)
    triu = jnp.asarray(same_chunk & (rows[None, :] >= rows[:, None]), BF16)
    main, cf, cb = _inproj_b(x, g[None], w_main, w_a, wg.astype(BF16), gate_bias.reshape(1, -1), tril, triu)
    o_f, o_b = _gla(main.reshape(nb, SEQ, -1), cf.reshape(nb, SEQ, -1), cb.reshape(nb, SEQ, -1))
    return _outproj_b(o_f.reshape(-1, D_INNER), o_b.reshape(-1, D_INNER), main, o_g[None],
                      w_out.astype(BF16), x)


def _layer_c(x, g, w_in, q_g, k_g, lam, o_g, w_out, lambda_init):
    nb = x.shape[0] // SEQ
    qkvz = _inproj_c(x, g[None], w_in.astype(BF16), q_g[None], k_g[None])
    sl_np = (2.0 ** (-8.0 * np.arange(1, DIFF_HEADS + 1) / DIFF_HEADS) * LOG2E).astype(np.float32)
    sl = jnp.asarray(sl_np)
    qkvz3 = qkvz.reshape(nb, SEQ, -1)
    logit_bound = jnp.max(jnp.abs(q_g)) * jnp.max(jnp.abs(k_g)) * (DIFF_DQK ** 0.5 * LOG2E * 1.02)
    o = lax.cond(logit_bound <= MAX_UNSHIFTED_LOG2,
                 lambda: _attention_bounded(qkvz3, sl_np, lam, o_g[None], lambda_init),
                 lambda: _attention_online(qkvz3, sl, lam, o_g[None], lambda_init))
    return _outproj_c(o.reshape(-1, D_INNER), qkvz, w_out.astype(BF16), x)


def kernel(x_prompt, x_sample, norm_g, a_w_in, a_v_g, a_w_s, a_b_s, a_w_out, b_w_in, b_w_gate, b_gate_bias,
           b_o_g, b_w_out, c_w_in, c_q_g, c_k_g, c_lam, c_o_g, c_w_out):
    n_prompt = x_prompt.shape[0] * x_prompt.shape[1]
    x = jnp.concatenate([x_prompt.reshape(-1, D_MODEL), x_sample.reshape(-1, D_MODEL)], axis=0)
    for i in range(DEPTH):
        kind, j = i % 3, i // 3
        if kind == 0:
            x = _layer_a(x, norm_g[i], a_w_in[j], a_v_g[j], a_w_s[j], a_b_s[j], a_w_out[j])
        elif kind == 1:
            x = _layer_b(x, norm_g[i], b_w_in[j], b_w_gate[j], b_gate_bias[j], b_o_g[j], b_w_out[j])
        else:
            lambda_init = 0.8 - 0.6 * math.exp(-0.3 * i)
            x = _layer_c(x, norm_g[i], c_w_in[j], c_q_g[j], c_k_g[j], c_lam[j], c_o_g[j], c_w_out[j],
                         lambda_init)
    return x[:n_prompt].reshape(x_prompt.shape), x[n_prompt:].reshape(x_sample.shape)
```

```python
import functools
import math

import jax
import jax.numpy as jnp
import numpy as np
from jax import lax
from jax.experimental import pallas as pl
from jax.experimental.pallas import tpu as pltpu

F32 = jnp.float32
BF16 = jnp.bfloat16

D_MODEL = 1024
SEQ = 16384
DEPTH = 4
D_INNER = 2048
EPS = 1e-6

SG_CHUNK = 128
SG_GROUPS = 8
SG_GDIM = D_INNER // SG_GROUPS

GLA_HEADS = 4
GLA_KEY = 512
GLA_DK = 128
GLA_DV = 512
GLA_RANK = 16
GLA_TAU = 16.0
GLA_CHUNK = 64

DIFF_HEADS = 8
DIFF_DQK = 128
DIFF_DV = 256

LANES = 128
VMEM_LIMIT = 56 * 1024 * 1024

TM = 512
GLA_ROWS = 256
TQ = 1024
TK = 512
TQ_ONLINE = 512
LOG2E = math.log2(math.e)
MAX_UNSHIFTED_LOG2 = 80.0


def _cparams(sem):
    return pltpu.CompilerParams(dimension_semantics=sem, vmem_limit_bytes=VMEM_LIMIT)


def _resident(shape):
    nd = len(shape)
    return pl.BlockSpec(shape, lambda *_: (0,) * nd, pipeline_mode=pl.Buffered(1))


def _rms(x, g):
    return x * lax.rsqrt(jnp.mean(x * x, axis=-1, keepdims=True) + EPS) * g


def _silu(z):
    return z * jax.nn.sigmoid(z)


def _dot(a, b):
    return jnp.dot(a, b, preferred_element_type=F32)


def _dot_nt(a, b):
    return lax.dot_general(a, b, (((1,), (1,)), ((), ())), preferred_element_type=F32)


def _dot_tn(a, b):
    return lax.dot_general(a, b, (((0,), (0,)), ((), ())), preferred_element_type=F32)


def _layer_a_kernel(x_ref, g_ref, w_in_ref, vg_ref, ws_ref, bs_ref, w_out_ref, o_ref, uvz_ref, y_ref):
    x = x_ref[...]
    h = _rms(x, g_ref[...]).astype(BF16)
    for blk in range(3):
        lo = blk * D_INNER
        y = _dot(h, w_in_ref[:, lo:lo + D_INNER])
        if blk == 1:
            y = _rms(y, vg_ref[...])
        uvz_ref[:, lo:lo + D_INNER] = y.astype(BF16)
    for c in range(TM // SG_CHUNK):
        rows = slice(c * SG_CHUNK, (c + 1) * SG_CHUNK)
        for grp in range(SG_GROUPS):
            u_cols = slice(grp * SG_GDIM, (grp + 1) * SG_GDIM)
            v_cols = slice(D_INNER + u_cols.start, D_INNER + u_cols.stop)
            z_cols = slice(2 * D_INNER + u_cols.start, 2 * D_INNER + u_cols.stop)
            sv = _dot(ws_ref[grp], uvz_ref[rows, v_cols]) + bs_ref[grp]
            y = uvz_ref[rows, u_cols].astype(F32) * sv * _silu(uvz_ref[rows, z_cols].astype(F32))
            y_ref[rows, u_cols] = y.astype(BF16)
    o_ref[...] = x + _dot(y_ref[...], w_out_ref[...])


def _layer_a_call(x, g, w_in, vg, ws, bs, w_out):
    t = x.shape[0]
    n = 3 * D_INNER
    return pl.pallas_call(
        _layer_a_kernel,
        grid=(t // TM,),
        in_specs=[pl.BlockSpec((TM, D_MODEL), lambda i: (i, 0)),
                  _resident((1, D_MODEL)), _resident((D_MODEL, n)), _resident((1, D_INNER)),
                  _resident((SG_GROUPS, SG_CHUNK, SG_CHUNK)), _resident((SG_GROUPS, SG_CHUNK, 1)),
                  _resident((D_INNER, D_MODEL))],
        out_specs=pl.BlockSpec((TM, D_MODEL), lambda i: (i, 0)),
        out_shape=jax.ShapeDtypeStruct((t, D_MODEL), F32),
        scratch_shapes=[pltpu.VMEM((TM, n), BF16), pltpu.VMEM((TM, D_INNER), BF16)],
        compiler_params=_cparams(("parallel",)),
        name="layer_a",
    )(x, g, w_in, vg, ws, bs, w_out)


def _log_sigmoid(x):
    return -(jnp.maximum(-x, 0.0) + jnp.log1p(jnp.exp(-jnp.abs(x))))


def _split3(x):
    hi = x.astype(BF16)
    r1 = x - hi.astype(F32)
    mid = r1.astype(BF16)
    lo = (r1 - mid.astype(F32)).astype(BF16)
    return hi, mid, lo


def _inproj_b_kernel(x_ref, g_ref, w_ref, wa_ref, wg_ref, gb_ref, tril_ref, triu_ref,
                     o_ref, cf_ref, cb_ref):
    h = _rms(x_ref[...], g_ref[...]).astype(BF16)
    for blk in range(2):
        lo = blk * D_INNER
        o_ref[:, lo:lo + D_INNER] = _dot(h, w_ref[:, lo:lo + D_INNER]).astype(BF16)
    lo = 2 * D_INNER
    qk = _dot(h, w_ref[:, lo:lo + 2 * GLA_KEY])
    o_ref[:, lo:lo + GLA_KEY] = (qk[:, :GLA_KEY] * GLA_DK ** -0.5).astype(BF16)
    o_ref[:, lo + GLA_KEY:lo + 2 * GLA_KEY] = qk[:, GLA_KEY:].astype(BF16)
    code = _dot(h, wa_ref[...]).astype(BF16)
    la = _log_sigmoid(_dot(code, wg_ref[...]) + gb_ref[...]) / GLA_TAU
    cf = jnp.zeros((TM, GLA_KEY), F32)
    cb = jnp.zeros((TM, GLA_KEY), F32)
    for part_f, part_b in zip(_split3(la[:, :GLA_KEY])[:2], _split3(la[:, GLA_KEY:])[:2]):
        cf = cf + _dot(tril_ref[...], part_f)
        cb = cb + _dot(triu_ref[...], part_b)
    cf_ref[...] = cf
    cb_ref[...] = cb


def _inproj_b(x, g, w_main, w_a, w_gate, gate_bias, tril, triu):
    t = x.shape[0]
    n = w_main.shape[1]
    return pl.pallas_call(
        _inproj_b_kernel,
        grid=(t // TM,),
        in_specs=[pl.BlockSpec((TM, D_MODEL), lambda i: (i, 0)),
                  _resident((1, D_MODEL)), _resident((D_MODEL, n)), _resident((D_MODEL, LANES)),
                  _resident((LANES, 2 * GLA_KEY)), _resident((1, 2 * GLA_KEY)),
                  _resident((TM, TM)), _resident((TM, TM))],
        out_specs=[pl.BlockSpec((TM, n), lambda i: (i, 0)),
                   pl.BlockSpec((TM, GLA_KEY), lambda i: (i, 0)),
                   pl.BlockSpec((TM, GLA_KEY), lambda i: (i, 0))],
        out_shape=[jax.ShapeDtypeStruct((t, n), BF16),
                   jax.ShapeDtypeStruct((t, GLA_KEY), F32),
                   jax.ShapeDtypeStruct((t, GLA_KEY), F32)],
        compiler_params=_cparams(("parallel",)),
        name="inproj_b",
    )(x, g, w_main, w_a, w_gate, gate_bias, tril, triu)


def _inproj_c_kernel(x_ref, g_ref, w_ref, qg_ref, kg_ref, o_ref):
    h = _rms(x_ref[...], g_ref[...]).astype(BF16)
    half = D_INNER // 2
    for blk in range(8):
        lo = blk * half
        y = _dot(h, w_ref[:, lo:lo + half])
        if blk < 4:
            gain = qg_ref[...] if blk < 2 else kg_ref[...]
            scale = DIFF_DQK ** -0.5 * LOG2E if blk < 2 else 1.0
            for grp in range(half // DIFF_DQK):
                sl = slice(grp * DIFF_DQK, (grp + 1) * DIFF_DQK)
                o_ref[:, lo + sl.start:lo + sl.stop] = (_rms(y[:, sl], gain) * scale).astype(BF16)
        else:
            o_ref[:, lo:lo + half] = y.astype(BF16)


def _inproj_c(x, g, w, qg, kg):
    t = x.shape[0]
    n = 4 * D_INNER
    return pl.pallas_call(
        _inproj_c_kernel,
        grid=(t // TM,),
        in_specs=[pl.BlockSpec((TM, D_MODEL), lambda i: (i, 0)),
                  _resident((1, D_MODEL)), _resident((D_MODEL, n)),
                  _resident((1, DIFF_DQK)), _resident((1, DIFF_DQK))],
        out_specs=pl.BlockSpec((TM, n), lambda i: (i, 0)),
        out_shape=jax.ShapeDtypeStruct((t, n), BF16),
        compiler_params=_cparams(("parallel",)),
        name="inproj_c",
    )(x, g, w, qg, kg)


def _outproj_b_kernel(of_ref, ob_ref, g_ref, og_ref, w_ref, x_ref, o_ref, y_ref):
    for hd in range(GLA_HEADS):
        cols = slice(hd * GLA_DV, (hd + 1) * GLA_DV)
        o = of_ref[:, cols].astype(F32) + ob_ref[:, cols].astype(F32)
        y = _rms(o, og_ref[...]) * _silu(g_ref[:, cols].astype(F32))
        y_ref[:, cols] = y.astype(BF16)
    o_ref[...] = x_ref[...] + _dot(y_ref[...], w_ref[...])


def _outproj_b(o_f, o_b, main, og, w, x):
    t = x.shape[0]
    return pl.pallas_call(
        _outproj_b_kernel,
        grid=(t // TM,),
        in_specs=[pl.BlockSpec((TM, D_INNER), lambda i: (i, 0)),
                  pl.BlockSpec((TM, D_INNER), lambda i: (i, 0)),
                  pl.BlockSpec((TM, D_INNER), lambda i: (i, 1)),
                  _resident((1, GLA_DV)), _resident((D_INNER, D_MODEL)),
                  pl.BlockSpec((TM, D_MODEL), lambda i: (i, 0))],
        out_specs=pl.BlockSpec((TM, D_MODEL), lambda i: (i, 0)),
        out_shape=jax.ShapeDtypeStruct((t, D_MODEL), F32),
        scratch_shapes=[pltpu.VMEM((TM, D_INNER), BF16)],
        compiler_params=_cparams(("parallel",)),
        name="outproj_b",
    )(o_f, o_b, main, og, w, x)


def _outproj_c_kernel(a_ref, z_ref, w_ref, x_ref, o_ref):
    y = (a_ref[...].astype(F32) * _silu(z_ref[...].astype(F32))).astype(BF16)
    o_ref[...] = x_ref[...] + _dot(y, w_ref[...])


def _outproj_c(o, qkvz, w, x):
    t = x.shape[0]
    return pl.pallas_call(
        _outproj_c_kernel,
        grid=(t // TM,),
        in_specs=[pl.BlockSpec((TM, D_INNER), lambda i: (i, 0)),
                  pl.BlockSpec((TM, D_INNER), lambda i: (i, 3)),
                  _resident((D_INNER, D_MODEL)),
                  pl.BlockSpec((TM, D_MODEL), lambda i: (i, 0))],
        out_specs=pl.BlockSpec((TM, D_MODEL), lambda i: (i, 0)),
        out_shape=jax.ShapeDtypeStruct((t, D_MODEL), F32),
        compiler_params=_cparams(("parallel",)),
        name="outproj_c",
    )(o, qkvz, w, x)


def _gla_chunk(q_ref, k_ref, c_ref, ct_ref, v_ref, o_ref, s_ref, hd, c, reverse):
    i_ref, i_last = (GLA_CHUNK // 2 - 1, 0) if reverse else (GLA_CHUNK // 2, GLA_CHUNK - 1)
    row = lax.broadcasted_iota(jnp.int32, (GLA_CHUNK, GLA_CHUNK), 0)
    col = lax.broadcasted_iota(jnp.int32, (GLA_CHUNK, GLA_CHUNK), 1)
    mask = (row < col) if reverse else (row >= col)
    rows = slice(c * GLA_CHUNK, (c + 1) * GLA_CHUNK)
    kcols = slice(hd * GLA_DK, (hd + 1) * GLA_DK)
    vcols = slice(hd * GLA_DV, (hd + 1) * GLA_DV)
    q = q_ref[rows, kcols].astype(F32)
    k = k_ref[rows, kcols].astype(F32)
    v = v_ref[rows, vcols]
    cum = c_ref[rows, kcols]
    ref = cum[i_ref:i_ref + 1, :]
    last = cum[i_last:i_last + 1, :]
    scores = _dot_nt((q * jnp.exp(cum - ref)).astype(BF16), (k * jnp.exp(ref - cum)).astype(BF16))
    scores = jnp.where(mask, scores, 0.0).astype(BF16)
    state = s_ref[hd]
    o = _dot(scores, v) + _dot((q * jnp.exp(cum)).astype(BF16), state.astype(BF16))
    o_ref[rows, vcols] = o.astype(o_ref.dtype)
    decay = jnp.exp(ct_ref[kcols, c * GLA_CHUNK + i_last:c * GLA_CHUNK + i_last + 1])
    s_ref[hd] = state * decay + _dot_tn((k * jnp.exp(last - cum)).astype(BF16), v)


def _gla_kernel(qf_ref, kf_ref, cf_ref, vf_ref, qb_ref, kb_ref, cb_ref, vb_ref,
                of_ref, ob_ref, sf_ref, sb_ref, ctf_ref, ctb_ref):
    @pl.when(pl.program_id(1) == 0)
    def _():
        sf_ref[...] = jnp.zeros_like(sf_ref)
        sb_ref[...] = jnp.zeros_like(sb_ref)

    ctf_ref[...] = cf_ref[...].T
    ctb_ref[...] = cb_ref[...].T
    n_chunks = GLA_ROWS // GLA_CHUNK
    for c in range(n_chunks):
        for hd in range(GLA_HEADS):
            _gla_chunk(qf_ref, kf_ref, cf_ref, ctf_ref, vf_ref, of_ref, sf_ref, hd, c, reverse=False)
            _gla_chunk(qb_ref, kb_ref, cb_ref, ctb_ref, vb_ref, ob_ref, sb_ref, hd, n_chunks - 1 - c,
                       reverse=True)


def _gla(main3, cf3, cb3):
    nb, s, _ = main3.shape
    steps = s // GLA_ROWS
    q_col = 2 * D_INNER // GLA_KEY
    k_col = q_col + 1

    def fwd(col):
        return lambda b, i: (b, i, col)

    def bwd(col):
        return lambda b, i: (b, steps - 1 - i, col)

    def specs(idx):
        return [pl.BlockSpec((None, GLA_ROWS, GLA_KEY), idx(q_col)),
                pl.BlockSpec((None, GLA_ROWS, GLA_KEY), idx(k_col)),
                pl.BlockSpec((None, GLA_ROWS, GLA_KEY), idx(0)),
                pl.BlockSpec((None, GLA_ROWS, D_INNER), idx(0))]

    out_sd = jax.ShapeDtypeStruct((nb, s, D_INNER), BF16)
    state = pltpu.VMEM((GLA_HEADS, GLA_DK, GLA_DV), F32)
    cum_t = pltpu.VMEM((GLA_KEY, GLA_ROWS), F32)
    return pl.pallas_call(
        _gla_kernel,
        grid=(nb, steps),
        in_specs=specs(fwd) + specs(bwd),
        out_specs=[pl.BlockSpec((None, GLA_ROWS, D_INNER), fwd(0)),
                   pl.BlockSpec((None, GLA_ROWS, D_INNER), bwd(0))],
        out_shape=[out_sd, out_sd],
        scratch_shapes=[state, state, cum_t, cum_t],
        compiler_params=_cparams(("parallel", "arbitrary")),
        name="gla_scan",
    )(main3, main3, cf3, main3, main3, main3, cb3, main3)


def _attn_finish(acc_ref, l0, l1, lam_ref, og_ref, o_ref, lambda_init):
    lam = lam_ref[...]
    lam_full = (jnp.exp(jnp.sum(lam[0:1] * lam[1:2], axis=1, keepdims=True))
                - jnp.exp(jnp.sum(lam[2:3] * lam[3:4], axis=1, keepdims=True)) + lambda_init)
    o = acc_ref[0] * (1.0 / l0) - lam_full * (acc_ref[1] * (1.0 / l1))
    o_ref[...] = (_rms(o, og_ref[...]) * (1.0 - lambda_init)).astype(o_ref.dtype)


def _lane_select(lane, pieces):
    out = 0.0
    for c, piece in reversed(list(enumerate(pieces))):
        out = jnp.where(lane == c, piece, out)
    return out


def _attn_bounded_kernel(sl_ref, q_ref, k_ref, v_ref, qx_ref, kx_ref, lam_ref, og_ref, o_ref,
                         qa_ref, p_ref, l_ref, acc_ref, *, lambda_init):
    hd = pl.program_id(1)
    qi = pl.program_id(2)
    ratio = TQ // TK
    n_off = k_ref.shape[0] // TK - ratio
    assert ratio == 2 and n_off >= 2 and n_off % 2 == 0
    sl = sl_ref[hd]

    for m in range(2):
        qa_ref[m, :, :DIFF_DQK] = q_ref[:, m * DIFF_DQK:(m + 1) * DIFF_DQK]
        qa_ref[m, :, DIFF_DQK:] = qx_ref[...]
    l_ref[...] = jnp.zeros_like(l_ref)
    acc_ref[...] = jnp.zeros_like(acc_ref)
    lane8 = lax.broadcasted_iota(jnp.int32, (8, LANES), 1)

    def tile_of(t):
        is_left = t < ratio * qi
        return is_left, jnp.where(is_left, t, t + ratio)

    def probabilities(slot, logits_of_map):
        for m in range(2):
            p = jnp.exp2(logits_of_map(m))
            part = p[:, :LANES]
            for c in range(1, TK // LANES):
                part = part + p[:, c * LANES:(c + 1) * LANES]
            l_ref[m] += part
            p_ref[slot, m] = p.astype(BF16)

    def off_diag(t, slot):
        is_left, j = tile_of(t)
        gap = jnp.where(is_left, qi * TQ - (j + 1) * TK, j * TK - (qi + 1) * TQ).astype(F32)
        gap_pieces = [p.astype(F32) for p in _split3(jnp.full((8, LANES), sl * gap, F32))]
        gap_row = _lane_select(lane8, [0.0] * 9 + gap_pieces)[0:1, :].astype(BF16)
        k_ext = kx_ref[jnp.where(is_left, 0, 1)] + gap_row
        k = k_ref[pl.ds(pl.multiple_of(j * TK, TK), TK), :]

        def logits(m):
            ka = jnp.concatenate([k[:, m * DIFF_DQK:(m + 1) * DIFF_DQK], k_ext], axis=1)
            return _dot_nt(qa_ref[m], ka)

        probabilities(slot, logits)

    def diag(d, slot):
        k = k_ref[pl.ds(pl.multiple_of((ratio * qi + d) * TK, TK), TK), :]
        i_loc = lax.broadcasted_iota(jnp.int32, (TQ, TK), 0)
        j_loc = lax.broadcasted_iota(jnp.int32, (TQ, TK), 1)
        bias = sl * jnp.abs(i_loc - j_loc - d * TK).astype(F32)

        def logits(m):
            cols = slice(m * DIFF_DQK, (m + 1) * DIFF_DQK)
            return _dot_nt(q_ref[:, cols], k[:, cols]) - bias

        probabilities(slot, logits)

    def weighted_values(j, slot):
        v = v_ref[pl.ds(pl.multiple_of(j * TK, TK), TK), :]
        for m in range(2):
            acc_ref[m] += _dot(p_ref[slot, m], v)

    off_diag(0, 0)

    def pair(u, carry):
        t = 2 * u
        off_diag(t + 1, 1)
        weighted_values(tile_of(t)[1], 0)
        off_diag(t + 2, 0)
        weighted_values(tile_of(t + 1)[1], 1)
        return carry

    lax.fori_loop(0, (n_off - 2) // 2, pair, 0, unroll=7)
    off_diag(n_off - 1, 1)
    weighted_values(tile_of(n_off - 2)[1], 0)
    diag(0, 0)
    weighted_values(tile_of(n_off - 1)[1], 1)
    diag(1, 1)
    weighted_values(ratio * qi, 0)
    weighted_values(ratio * qi + 1, 1)

    _attn_finish(acc_ref, jnp.sum(l_ref[0], axis=1, keepdims=True), jnp.sum(l_ref[1], axis=1, keepdims=True),
                 lam_ref, og_ref, o_ref, lambda_init)


def _split3_np(x):
    x = np.asarray(x, np.float32)
    hi = x.astype(BF16).astype(np.float32)
    mid = (x - hi).astype(BF16).astype(np.float32)
    lo = (x - hi - mid).astype(BF16).astype(np.float32)
    return [hi, mid, lo]


def _alibi_tables(sl_np):
    nh = len(sl_np)
    ii = np.arange(TQ, dtype=np.float32)
    jj = np.arange(TK, dtype=np.float32)
    qx = np.zeros((nh, TQ, LANES), np.float32)
    kx = np.zeros((nh, 2, TK, LANES), np.float32)
    for h, s in enumerate(np.asarray(sl_np, np.float32)):
        qx[h, :, 0:3] = -1.0
        qx[h, :, 9:12] = -1.0
        for c, piece in enumerate(_split3_np(s * ii)):
            qx[h, :, 3 + c] = piece
        for c, piece in enumerate(_split3_np(s * (TQ - ii))):
            qx[h, :, 6 + c] = piece
        for c, piece in enumerate(_split3_np(s * (TK - jj))):
            kx[h, 0, :, c] = piece
        for c, piece in enumerate(_split3_np(s * jj)):
            kx[h, 1, :, c] = piece
        kx[h, 0, :, 3:6] = -1.0
        kx[h, 1, :, 6:9] = -1.0
    return jnp.asarray(qx, BF16), jnp.asarray(kx, BF16)


def _attention_bounded(qkvz3, sl_np, lam, og, lambda_init):
    nb, s, _ = qkvz3.shape
    k_col0 = D_INNER // DIFF_DV
    v_col0 = 2 * k_col0
    qx, kx = _alibi_tables(sl_np)
    return pl.pallas_call(
        functools.partial(_attn_bounded_kernel, lambda_init=lambda_init),
        grid=(nb, DIFF_HEADS, s // TQ),
        in_specs=[pl.BlockSpec(memory_space=pltpu.SMEM),
                  pl.BlockSpec((None, TQ, DIFF_DV), lambda b, h, i: (b, i, h)),
                  pl.BlockSpec((None, s, DIFF_DV), lambda b, h, i: (b, 0, k_col0 + h)),
                  pl.BlockSpec((None, s, DIFF_DV), lambda b, h, i: (b, 0, v_col0 + h)),
                  pl.BlockSpec((None, TQ, LANES), lambda b, h, i: (h, 0, 0)),
                  pl.BlockSpec((None, 2, TK, LANES), lambda b, h, i: (h, 0, 0, 0)),
                  _resident((4, DIFF_DQK)), _resident((1, DIFF_DV))],
        out_specs=pl.BlockSpec((None, TQ, DIFF_DV), lambda b, h, i: (b, i, h)),
        out_shape=jax.ShapeDtypeStruct((nb, s, D_INNER), BF16),
        scratch_shapes=[pltpu.VMEM((2, TQ, 2 * DIFF_DQK), BF16),
                        pltpu.VMEM((2, 2, TQ, TK), BF16),
                        pltpu.VMEM((2, TQ, LANES), F32),
                        pltpu.VMEM((2, TQ, DIFF_DV), F32)],
        compiler_params=_cparams(("parallel", "parallel", "arbitrary")),
        name="diff_attention_bounded",
    )(jnp.asarray(sl_np, F32), qkvz3, qkvz3, qkvz3, qx, kx, lam, og)


def _attn_online_kernel(sl_ref, q_ref, k_ref, v_ref, absd_ref, kx_ref, lam_ref, og_ref, o_ref,
                        ql_ref, qr_ref, m_ref, l_ref, acc_ref, *, lambda_init):
    TQ = TQ_ONLINE
    hd = pl.program_id(1)
    qi = pl.program_id(2)
    n_kv = k_ref.shape[0] // TQ
    slope = sl_ref[hd]

    lane = lax.broadcasted_iota(jnp.int32, (TQ, LANES), 1)
    pieces = [p.astype(F32) for p in _split3(jnp.full((TQ, LANES), slope, F32))]
    ext = _lane_select(lane, pieces + pieces).astype(BF16)
    for m in range(2):
        qm = q_ref[:, m * DIFF_DQK:(m + 1) * DIFF_DQK]
        ql_ref[m, :, :DIFF_DQK] = qm
        ql_ref[m, :, DIFF_DQK:] = ext
        qr_ref[m, :, :DIFF_DQK] = qm
        qr_ref[m, :, DIFF_DQK:] = -ext
    m_ref[...] = jnp.full(m_ref.shape, -1e30, F32)
    l_ref[...] = jnp.zeros_like(l_ref)
    acc_ref[...] = jnp.zeros_like(acc_ref)
    ii = lax.broadcasted_iota(jnp.int32, (TQ, 1), 0).astype(F32)

    def update(m, s, r, v):
        m_old = m_ref[m]
        m_new = jnp.maximum(m_old, jnp.max(s, axis=1, keepdims=True) + r)
        alpha = jnp.exp2(m_old - m_new)
        p = jnp.exp2(s - (m_new - r))
        l_ref[m] = alpha * l_ref[m] + jnp.sum(p, axis=1, keepdims=True)
        acc_ref[m] = alpha * acc_ref[m] + _dot(p.astype(BF16), v)
        m_ref[m] = m_new

    def off_diag(j, qa_ref, r):
        start = pl.multiple_of(j * TQ, TQ)
        k = k_ref[pl.ds(start, TQ), :]
        v = v_ref[pl.ds(start, TQ), :]
        for m in range(2):
            ka = jnp.concatenate([k[:, m * DIFF_DQK:(m + 1) * DIFF_DQK], kx_ref[...]], axis=1)
            update(m, _dot_nt(qa_ref[m], ka), r, v)

    def left(j, carry):
        off_diag(j, ql_ref, -slope * (((qi - j) * TQ).astype(F32) + ii))
        return carry

    def right(j, carry):
        off_diag(j, qr_ref, slope * (ii - ((j - qi) * TQ).astype(F32)))
        return carry

    lax.fori_loop(0, qi, left, 0)

    start = pl.multiple_of(qi * TQ, TQ)
    k = k_ref[pl.ds(start, TQ), :]
    v = v_ref[pl.ds(start, TQ), :]
    bias = -slope * absd_ref[...]
    zero = jnp.zeros((TQ, 1), F32)
    for m in range(2):
        cols = slice(m * DIFF_DQK, (m + 1) * DIFF_DQK)
        update(m, _dot_nt(q_ref[:, cols], k[:, cols]) + bias, zero, v)

    lax.fori_loop(qi + 1, n_kv, right, 0)

    _attn_finish(acc_ref, l_ref[0], l_ref[1], lam_ref, og_ref, o_ref, lambda_init)


def _attention_online(qkvz3, sl, lam, og, lambda_init):
    TQ = TQ_ONLINE
    nb, s, _ = qkvz3.shape
    k_col0 = D_INNER // DIFF_DV
    v_col0 = 2 * k_col0
    pos = np.arange(TQ)
    absd = jnp.asarray(np.abs(pos[:, None] - pos[None, :]), F32)
    kx_np = np.zeros((TQ, LANES), np.float32)
    kx_np[:, 0:3] = ((pos // 16) * 16)[:, None]
    kx_np[:, 3:6] = (pos % 16)[:, None]
    kx = jnp.asarray(kx_np, BF16)
    return pl.pallas_call(
        functools.partial(_attn_online_kernel, lambda_init=lambda_init),
        grid=(nb, DIFF_HEADS, s // TQ),
        in_specs=[pl.BlockSpec(memory_space=pltpu.SMEM),
                  pl.BlockSpec((None, TQ, DIFF_DV), lambda b, h, i: (b, i, h)),
                  pl.BlockSpec((None, s, DIFF_DV), lambda b, h, i: (b, 0, k_col0 + h)),
                  pl.BlockSpec((None, s, DIFF_DV), lambda b, h, i: (b, 0, v_col0 + h)),
                  _resident((TQ, TQ)), _resident((TQ, LANES)),
                  _resident((4, DIFF_DQK)), _resident((1, DIFF_DV))],
        out_specs=pl.BlockSpec((None, TQ, DIFF_DV), lambda b, h, i: (b, i, h)),
        out_shape=jax.ShapeDtypeStruct((nb, s, D_INNER), BF16),
        scratch_shapes=[pltpu.VMEM((2, TQ, 2 * DIFF_DQK), BF16), pltpu.VMEM((2, TQ, 2 * DIFF_DQK), BF16),
                        pltpu.VMEM((2, TQ, 1), F32), pltpu.VMEM((2, TQ, 1), F32),
                        pltpu.VMEM((2, TQ, DIFF_DV), F32)],
        compiler_params=_cparams(("parallel", "parallel", "arbitrary")),
        name="diff_attention_online",
    )(sl, qkvz3, qkvz3, qkvz3, absd, kx, lam, og)


def _prep_a(g, w_in, v_g, w_s, b_s, w_out):
    return (g[None], w_in.astype(BF16), v_g[None], w_s.astype(BF16), b_s[:, :, None], w_out.astype(BF16))


def _prep_b(g, w_in, w_gate, gate_bias, o_g, w_out):
    c_q, c_k, c_v, c_g, c_a = 0, GLA_KEY, 2 * GLA_KEY, 2 * GLA_KEY + D_INNER, 2 * GLA_KEY + 2 * D_INNER
    w_main = jnp.concatenate([w_in[:, c_v:c_g], w_in[:, c_g:c_a], w_in[:, c_q:c_k], w_in[:, c_k:c_v]],
                             axis=1).astype(BF16)
    w_a = jnp.pad(w_in[:, c_a:], ((0, 0), (0, LANES - 2 * GLA_RANK))).astype(BF16)
    wg = jnp.zeros((LANES, 2 * GLA_KEY), F32)
    wg = wg.at[:GLA_RANK, :GLA_KEY].set(w_gate[0]).at[GLA_RANK:2 * GLA_RANK, GLA_KEY:].set(w_gate[1])
    rows = np.arange(TM)
    same_chunk = (rows[:, None] // GLA_CHUNK) == (rows[None, :] // GLA_CHUNK)
    tril = jnp.asarray(same_chunk & (rows[None, :] <= rows[:, None]), BF16)
    triu = jnp.asarray(same_chunk & (rows[None, :] >= rows[:, None]), BF16)
    return (g[None], w_main, w_a, wg.astype(BF16), gate_bias.reshape(1, -1), tril, triu, o_g[None],
            w_out.astype(BF16))


def _layer_b(x, g, w_main, w_a, wg, gate_bias, tril, triu, o_g, w_out):
    nb = x.shape[0] // SEQ
    main, cf, cb = _inproj_b(x, g, w_main, w_a, wg, gate_bias, tril, triu)
    o_f, o_b = _gla(main.reshape(nb, SEQ, -1), cf.reshape(nb, SEQ, -1), cb.reshape(nb, SEQ, -1))
    return _outproj_b(o_f.reshape(-1, D_INNER), o_b.reshape(-1, D_INNER), main, o_g, w_out, x)


def _prep_c(g, w_in, q_g, k_g, lam, o_g, w_out, lambda_init):
    logit_bound = jnp.max(jnp.abs(q_g)) * jnp.max(jnp.abs(k_g)) * (DIFF_DQK ** 0.5 * LOG2E * 1.02)
    return (g[None], w_in.astype(BF16), q_g[None], k_g[None], lam, o_g[None], w_out.astype(BF16),
            logit_bound <= MAX_UNSHIFTED_LOG2, lambda_init)


def _layer_c(x, g, w_in, q_g, k_g, lam, o_g, w_out, logits_bounded, lambda_init):
    nb = x.shape[0] // SEQ
    qkvz = _inproj_c(x, g, w_in, q_g, k_g)
    sl_np = (2.0 ** (-8.0 * np.arange(1, DIFF_HEADS + 1) / DIFF_HEADS) * LOG2E).astype(np.float32)
    qkvz3 = qkvz.reshape(nb, SEQ, -1)
    o = lax.cond(logits_bounded,
                 lambda: _attention_bounded(qkvz3, sl_np, lam, o_g, lambda_init),
                 lambda: _attention_online(qkvz3, jnp.asarray(sl_np), lam, o_g, lambda_init))
    return _outproj_c(o.reshape(-1, D_INNER), qkvz, w_out, x)


def kernel(x_prompt, x_sample, norm_g, a_w_in, a_v_g, a_w_s, a_b_s, a_w_out, b_w_in, b_w_gate, b_gate_bias,
           b_o_g, b_w_out, c_w_in, c_q_g, c_k_g, c_lam, c_o_g, c_w_out):
    layers = []
    for i in range(DEPTH):
        kind, j = i % 3, i // 3
        if kind == 0:
            layers.append((_layer_a_call, _prep_a(norm_g[i], a_w_in[j], a_v_g[j], a_w_s[j], a_b_s[j], a_w_out[j])))
        elif kind == 1:
            layers.append((_layer_b, _prep_b(norm_g[i], b_w_in[j], b_w_gate[j], b_gate_bias[j], b_o_g[j],
                                             b_w_out[j])))
        else:
            lambda_init = 0.8 - 0.6 * math.exp(-0.3 * i)
            layers.append((_layer_c, _prep_c(norm_g[i], c_w_in[j], c_q_g[j], c_k_g[j], c_lam[j], c_o_g[j],
                                             c_w_out[j], lambda_init)))

    def trunk(x_group):
        x = x_group.reshape(-1, D_MODEL)
        for layer, params in layers:
            x = layer(x, *params)
        return x.reshape(x_group.shape)

    return trunk(x_prompt), trunk(x_sample)
```

```python
import functools
import math

import jax
import jax.numpy as jnp
import numpy as np
from jax import lax
from jax.experimental import pallas as pl
from jax.experimental.pallas import tpu as pltpu

F32 = jnp.float32
BF16 = jnp.bfloat16

D_MODEL = 1024
SEQ = 16384
DEPTH = 4
D_INNER = 2048
EPS = 1e-6

SG_CHUNK = 128
SG_GROUPS = 8
SG_GDIM = D_INNER // SG_GROUPS

GLA_HEADS = 4
GLA_KEY = 512
GLA_DK = 128
GLA_DV = 512
GLA_RANK = 16
GLA_TAU = 16.0
GLA_CHUNK = 64

DIFF_HEADS = 8
DIFF_DQK = 128
DIFF_DV = 256

LANES = 128
VMEM_LIMIT = 56 * 1024 * 1024

TM = 512
GLA_ROWS = 256
TQ = 1024
TK = 512
TQ_ONLINE = 512
LOG2E = math.log2(math.e)
MAX_UNSHIFTED_LOG2 = 80.0


def _cparams(sem):
    return pltpu.CompilerParams(dimension_semantics=sem, vmem_limit_bytes=VMEM_LIMIT)


def _resident(shape):
    nd = len(shape)
    return pl.BlockSpec(shape, lambda *_: (0,) * nd, pipeline_mode=pl.Buffered(1))


def _rms(x, g):
    return x * lax.rsqrt(jnp.mean(x * x, axis=-1, keepdims=True) + EPS) * g


def _silu(z):
    return z * jax.nn.sigmoid(z)


def _dot(a, b):
    return jnp.dot(a, b, preferred_element_type=F32)


def _dot_nt(a, b):
    return lax.dot_general(a, b, (((1,), (1,)), ((), ())), preferred_element_type=F32)


def _dot_tn(a, b):
    return lax.dot_general(a, b, (((0,), (0,)), ((), ())), preferred_element_type=F32)


def _layer_a_kernel(x_ref, g_ref, w_in_ref, vg_ref, ws_ref, bs_ref, w_out_ref, o_ref, uvz_ref, y_ref):
    x = x_ref[...]
    h = _rms(x, g_ref[...]).astype(BF16)
    for blk in range(3):
        lo = blk * D_INNER
        y = _dot(h, w_in_ref[:, lo:lo + D_INNER])
        if blk == 1:
            y = _rms(y, vg_ref[...])
        uvz_ref[:, lo:lo + D_INNER] = y.astype(BF16)
    for c in range(TM // SG_CHUNK):
        rows = slice(c * SG_CHUNK, (c + 1) * SG_CHUNK)
        for grp in range(SG_GROUPS):
            u_cols = slice(grp * SG_GDIM, (grp + 1) * SG_GDIM)
            v_cols = slice(D_INNER + u_cols.start, D_INNER + u_cols.stop)
            z_cols = slice(2 * D_INNER + u_cols.start, 2 * D_INNER + u_cols.stop)
            sv = _dot(ws_ref[grp], uvz_ref[rows, v_cols]) + bs_ref[grp]
            y = uvz_ref[rows, u_cols].astype(F32) * sv * _silu(uvz_ref[rows, z_cols].astype(F32))
            y_ref[rows, u_cols] = y.astype(BF16)
    o_ref[...] = x + _dot(y_ref[...], w_out_ref[...])


def _layer_a_call(x, g, w_in, vg, ws, bs, w_out):
    t = x.shape[0]
    n = 3 * D_INNER
    return pl.pallas_call(
        _layer_a_kernel,
        grid=(t // TM,),
        in_specs=[pl.BlockSpec((TM, D_MODEL), lambda i: (i, 0)),
                  _resident((1, D_MODEL)), _resident((D_MODEL, n)), _resident((1, D_INNER)),
                  _resident((SG_GROUPS, SG_CHUNK, SG_CHUNK)), _resident((SG_GROUPS, SG_CHUNK, 1)),
                  _resident((D_INNER, D_MODEL))],
        out_specs=pl.BlockSpec((TM, D_MODEL), lambda i: (i, 0)),
        out_shape=jax.ShapeDtypeStruct((t, D_MODEL), F32),
        scratch_shapes=[pltpu.VMEM((TM, n), BF16), pltpu.VMEM((TM, D_INNER), BF16)],
        compiler_params=_cparams(("parallel",)),
        name="layer_a",
    )(x, g, w_in, vg, ws, bs, w_out)


def _log_sigmoid(x):
    return -(jnp.maximum(-x, 0.0) + jnp.log1p(jnp.exp(-jnp.abs(x))))


def _split3(x):
    hi = x.astype(BF16)
    r1 = x - hi.astype(F32)
    mid = r1.astype(BF16)
    lo = (r1 - mid.astype(F32)).astype(BF16)
    return hi, mid, lo


def _inproj_b_kernel(x_ref, g_ref, w_ref, wa_ref, wg_ref, gb_ref, tril_ref, triu_ref,
                     o_ref, cf_ref, cb_ref):
    h = _rms(x_ref[...], g_ref[...]).astype(BF16)
    for blk in range(2):
        lo = blk * D_INNER
        o_ref[:, lo:lo + D_INNER] = _dot(h, w_ref[:, lo:lo + D_INNER]).astype(BF16)
    lo = 2 * D_INNER
    qk = _dot(h, w_ref[:, lo:lo + 2 * GLA_KEY])
    o_ref[:, lo:lo + GLA_KEY] = (qk[:, :GLA_KEY] * GLA_DK ** -0.5).astype(BF16)
    o_ref[:, lo + GLA_KEY:lo + 2 * GLA_KEY] = qk[:, GLA_KEY:].astype(BF16)
    code = _dot(h, wa_ref[...]).astype(BF16)
    la = _log_sigmoid(_dot(code, wg_ref[...]) + gb_ref[...]) / GLA_TAU
    cf = jnp.zeros((TM, GLA_KEY), F32)
    cb = jnp.zeros((TM, GLA_KEY), F32)
    for part_f, part_b in zip(_split3(la[:, :GLA_KEY])[:2], _split3(la[:, GLA_KEY:])[:2]):
        cf = cf + _dot(tril_ref[...], part_f)
        cb = cb + _dot(triu_ref[...], part_b)
    cf_ref[...] = cf
    cb_ref[...] = cb


def _inproj_b(x, g, w_main, w_a, w_gate, gate_bias, tril, triu):
    t = x.shape[0]
    n = w_main.shape[1]
    return pl.pallas_call(
        _inproj_b_kernel,
        grid=(t // TM,),
        in_specs=[pl.BlockSpec((TM, D_MODEL), lambda i: (i, 0)),
                  _resident((1, D_MODEL)), _resident((D_MODEL, n)), _resident((D_MODEL, LANES)),
                  _resident((LANES, 2 * GLA_KEY)), _resident((1, 2 * GLA_KEY)),
                  _resident((TM, TM)), _resident((TM, TM))],
        out_specs=[pl.BlockSpec((TM, n), lambda i: (i, 0)),
                   pl.BlockSpec((TM, GLA_KEY), lambda i: (i, 0)),
                   pl.BlockSpec((TM, GLA_KEY), lambda i: (i, 0))],
        out_shape=[jax.ShapeDtypeStruct((t, n), BF16),
                   jax.ShapeDtypeStruct((t, GLA_KEY), F32),
                   jax.ShapeDtypeStruct((t, GLA_KEY), F32)],
        compiler_params=_cparams(("parallel",)),
        name="inproj_b",
    )(x, g, w_main, w_a, w_gate, gate_bias, tril, triu)


def _inproj_c_kernel(x_ref, g_ref, w_ref, qg_ref, kg_ref, o_ref):
    h = _rms(x_ref[...], g_ref[...]).astype(BF16)
    half = D_INNER // 2
    for blk in range(8):
        lo = blk * half
        y = _dot(h, w_ref[:, lo:lo + half])
        if blk < 4:
            gain = qg_ref[...] if blk < 2 else kg_ref[...]
            scale = DIFF_DQK ** -0.5 * LOG2E if blk < 2 else 1.0
            for grp in range(half // DIFF_DQK):
                sl = slice(grp * DIFF_DQK, (grp + 1) * DIFF_DQK)
                o_ref[:, lo + sl.start:lo + sl.stop] = (_rms(y[:, sl], gain) * scale).astype(BF16)
        else:
            o_ref[:, lo:lo + half] = y.astype(BF16)


def _inproj_c(x, g, w, qg, kg):
    t = x.shape[0]
    n = 4 * D_INNER
    return pl.pallas_call(
        _inproj_c_kernel,
        grid=(t // TM,),
        in_specs=[pl.BlockSpec((TM, D_MODEL), lambda i: (i, 0)),
                  _resident((1, D_MODEL)), _resident((D_MODEL, n)),
                  _resident((1, DIFF_DQK)), _resident((1, DIFF_DQK))],
        out_specs=pl.BlockSpec((TM, n), lambda i: (i, 0)),
        out_shape=jax.ShapeDtypeStruct((t, n), BF16),
        compiler_params=_cparams(("parallel",)),
        name="inproj_c",
    )(x, g, w, qg, kg)


def _outproj_b_kernel(of_ref, ob_ref, g_ref, og_ref, w_ref, x_ref, o_ref, y_ref):
    for hd in range(GLA_HEADS):
        cols = slice(hd * GLA_DV, (hd + 1) * GLA_DV)
        o = of_ref[:, cols].astype(F32) + ob_ref[:, cols].astype(F32)
        y = _rms(o, og_ref[...]) * _silu(g_ref[:, cols].astype(F32))
        y_ref[:, cols] = y.astype(BF16)
    o_ref[...] = x_ref[...] + _dot(y_ref[...], w_ref[...])


def _outproj_b(o_f, o_b, main, og, w, x):
    t = x.shape[0]
    return pl.pallas_call(
        _outproj_b_kernel,
        grid=(t // TM,),
        in_specs=[pl.BlockSpec((TM, D_INNER), lambda i: (i, 0)),
                  pl.BlockSpec((TM, D_INNER), lambda i: (i, 0)),
                  pl.BlockSpec((TM, D_INNER), lambda i: (i, 1)),
                  _resident((1, GLA_DV)), _resident((D_INNER, D_MODEL)),
                  pl.BlockSpec((TM, D_MODEL), lambda i: (i, 0))],
        out_specs=pl.BlockSpec((TM, D_MODEL), lambda i: (i, 0)),
        out_shape=jax.ShapeDtypeStruct((t, D_MODEL), F32),
        scratch_shapes=[pltpu.VMEM((TM, D_INNER), BF16)],
        compiler_params=_cparams(("parallel",)),
        name="outproj_b",
    )(o_f, o_b, main, og, w, x)


def _outproj_c_kernel(a_ref, z_ref, w_ref, x_ref, o_ref):
    y = (a_ref[...].astype(F32) * _silu(z_ref[...].astype(F32))).astype(BF16)
    o_ref[...] = x_ref[...] + _dot(y, w_ref[...])


def _outproj_c(o, qkvz, w, x):
    t = x.shape[0]
    return pl.pallas_call(
        _outproj_c_kernel,
        grid=(t // TM,),
        in_specs=[pl.BlockSpec((TM, D_INNER), lambda i: (i, 0)),
                  pl.BlockSpec((TM, D_INNER), lambda i: (i, 3)),
                  _resident((D_INNER, D_MODEL)),
                  pl.BlockSpec((TM, D_MODEL), lambda i: (i, 0))],
        out_specs=pl.BlockSpec((TM, D_MODEL), lambda i: (i, 0)),
        out_shape=jax.ShapeDtypeStruct((t, D_MODEL), F32),
        compiler_params=_cparams(("parallel",)),
        name="outproj_c",
    )(o, qkvz, w, x)


def _gla_pair(q_ref, k_ref, c_ref, ct_ref, v_ref, o_ref, s_ref, hd, pair, reverse):
    i_ref, i_last = (GLA_CHUNK // 2 - 1, 0) if reverse else (GLA_CHUNK // 2, GLA_CHUNK - 1)
    kcols = slice(hd * GLA_DK, (hd + 1) * GLA_DK)
    vcols = slice(hd * GLA_DV, (hd + 1) * GLA_DV)
    r0 = pair * 2 * GLA_CHUNK
    first, second = slice(r0, r0 + GLA_CHUNK), slice(r0 + GLA_CHUNK, r0 + 2 * GLA_CHUNK)
    rows_a, rows_b = (second, first) if reverse else (first, second)

    def in_memory_order(of_a, of_b):
        return jnp.concatenate([of_b, of_a] if reverse else [of_a, of_b], axis=0)

    def prepare(rows):
        q = q_ref[rows, kcols].astype(F32)
        k = k_ref[rows, kcols].astype(F32)
        cum = c_ref[rows, kcols]
        ref = cum[i_ref:i_ref + 1, :]
        last = cum[i_last:i_last + 1, :]
        return (q * jnp.exp(cum - ref), k * jnp.exp(ref - cum), q * jnp.exp(cum), k * jnp.exp(last - cum),
                jnp.exp(last))

    qg_a, kg_a, qi_a, ki_a, d_a = prepare(rows_a)
    qg_b, kg_b, qi_b, ki_b, d_b = prepare(rows_b)
    v_pair = v_ref[r0:r0 + 2 * GLA_CHUNK, vcols]
    zeros = jnp.zeros((GLA_CHUNK, GLA_DK), BF16)

    keys_for_a = in_memory_order(kg_a.astype(BF16), kg_b.astype(BF16))
    w_a = _dot_nt(qg_a.astype(BF16), keys_for_a)
    keys_for_b = in_memory_order(jnp.concatenate([zeros, ki_a.astype(BF16)], axis=1),
                                 jnp.concatenate([kg_b.astype(BF16), zeros], axis=1))
    w_b = _dot_nt(jnp.concatenate([qg_b, qi_b], axis=1).astype(BF16), keys_for_b)
    row = lax.broadcasted_iota(jnp.int32, (GLA_CHUNK, 2 * GLA_CHUNK), 0)
    col = lax.broadcasted_iota(jnp.int32, (GLA_CHUNK, 2 * GLA_CHUNK), 1)
    if reverse:
        keep_a = (col >= GLA_CHUNK) & (row < col - GLA_CHUNK)
        keep_b = (col >= GLA_CHUNK) | (row < col)
    else:
        keep_a = (col < GLA_CHUNK) & (row >= col)
        keep_b = (col < GLA_CHUNK) | (row >= col - GLA_CHUNK)
    w_a = jnp.where(keep_a, w_a, 0.0).astype(BF16)
    w_b = jnp.where(keep_b, w_b, 0.0).astype(BF16)

    state = s_ref[hd]
    lhs = jnp.concatenate([in_memory_order(qi_a.astype(BF16), (qi_b * d_a).astype(BF16)),
                           in_memory_order(w_a, w_b)], axis=1)
    rhs = jnp.concatenate([state.astype(BF16), v_pair], axis=0)
    o_ref[r0:r0 + 2 * GLA_CHUNK, vcols] = _dot(lhs, rhs).astype(o_ref.dtype)

    last_a = rows_a.start + i_last
    last_b = rows_b.start + i_last
    decay = jnp.exp(ct_ref[kcols, last_a:last_a + 1] + ct_ref[kcols, last_b:last_b + 1])
    keys_to_state = in_memory_order((ki_a * d_b).astype(BF16), ki_b.astype(BF16))
    s_ref[hd] = state * decay + _dot_tn(keys_to_state, v_pair)


def _gla_kernel(qf_ref, kf_ref, cf_ref, vf_ref, qb_ref, kb_ref, cb_ref, vb_ref,
                of_ref, ob_ref, sf_ref, sb_ref, ctf_ref, ctb_ref):
    @pl.when(pl.program_id(1) == 0)
    def _():
        sf_ref[...] = jnp.zeros_like(sf_ref)
        sb_ref[...] = jnp.zeros_like(sb_ref)

    ctf_ref[...] = cf_ref[...].T
    ctb_ref[...] = cb_ref[...].T
    n_pairs = GLA_ROWS // (2 * GLA_CHUNK)
    for p in range(n_pairs):
        for hd in range(GLA_HEADS):
            _gla_pair(qf_ref, kf_ref, cf_ref, ctf_ref, vf_ref, of_ref, sf_ref, hd, p, reverse=False)
            _gla_pair(qb_ref, kb_ref, cb_ref, ctb_ref, vb_ref, ob_ref, sb_ref, hd, n_pairs - 1 - p,
                      reverse=True)


def _gla(main3, cf3, cb3):
    nb, s, _ = main3.shape
    steps = s // GLA_ROWS
    q_col = 2 * D_INNER // GLA_KEY
    k_col = q_col + 1

    def fwd(col):
        return lambda b, i: (b, i, col)

    def bwd(col):
        return lambda b, i: (b, steps - 1 - i, col)

    def specs(idx):
        return [pl.BlockSpec((None, GLA_ROWS, GLA_KEY), idx(q_col)),
                pl.BlockSpec((None, GLA_ROWS, GLA_KEY), idx(k_col)),
                pl.BlockSpec((None, GLA_ROWS, GLA_KEY), idx(0)),
                pl.BlockSpec((None, GLA_ROWS, D_INNER), idx(0))]

    out_sd = jax.ShapeDtypeStruct((nb, s, D_INNER), BF16)
    state = pltpu.VMEM((GLA_HEADS, GLA_DK, GLA_DV), F32)
    cum_t = pltpu.VMEM((GLA_KEY, GLA_ROWS), F32)
    return pl.pallas_call(
        _gla_kernel,
        grid=(nb, steps),
        in_specs=specs(fwd) + specs(bwd),
        out_specs=[pl.BlockSpec((None, GLA_ROWS, D_INNER), fwd(0)),
                   pl.BlockSpec((None, GLA_ROWS, D_INNER), bwd(0))],
        out_shape=[out_sd, out_sd],
        scratch_shapes=[state, state, cum_t, cum_t],
        compiler_params=_cparams(("parallel", "arbitrary")),
        name="gla_scan",
    )(main3, main3, cf3, main3, main3, main3, cb3, main3)


def _attn_finish(acc_ref, l0, l1, lam_ref, og_ref, o_ref, lambda_init):
    lam = lam_ref[...]
    lam_full = (jnp.exp(jnp.sum(lam[0:1] * lam[1:2], axis=1, keepdims=True))
                - jnp.exp(jnp.sum(lam[2:3] * lam[3:4], axis=1, keepdims=True)) + lambda_init)
    o = acc_ref[0] * (1.0 / l0) - lam_full * (acc_ref[1] * (1.0 / l1))
    o_ref[...] = (_rms(o, og_ref[...]) * (1.0 - lambda_init)).astype(o_ref.dtype)


def _lane_select(lane, pieces):
    out = 0.0
    for c, piece in reversed(list(enumerate(pieces))):
        out = jnp.where(lane == c, piece, out)
    return out


def _attn_bounded_kernel(sl_ref, q_ref, k_ref, v_ref, qx_ref, kx_ref, lam_ref, og_ref, o_ref,
                         qa_ref, p_ref, l_ref, acc_ref, *, lambda_init):
    hd = pl.program_id(1)
    qi = pl.program_id(2)
    ratio = TQ // TK
    n_off = k_ref.shape[0] // TK - ratio
    assert ratio == 2 and n_off >= 2 and n_off % 2 == 0
    sl = sl_ref[hd]

    for m in range(2):
        qa_ref[m, :, :DIFF_DQK] = q_ref[:, m * DIFF_DQK:(m + 1) * DIFF_DQK]
        qa_ref[m, :, DIFF_DQK:] = qx_ref[...]
    l_ref[...] = jnp.zeros_like(l_ref)
    acc_ref[...] = jnp.zeros_like(acc_ref)
    lane8 = lax.broadcasted_iota(jnp.int32, (8, LANES), 1)

    def tile_of(t):
        is_left = t < ratio * qi
        return is_left, jnp.where(is_left, t, t + ratio)

    def probabilities(slot, logits_of_map):
        for m in range(2):
            p = jnp.exp2(logits_of_map(m))
            part = p[:, :LANES]
            for c in range(1, TK // LANES):
                part = part + p[:, c * LANES:(c + 1) * LANES]
            l_ref[m] += part
            p_ref[slot, m] = p.astype(BF16)

    def off_diag(t, slot):
        is_left, j = tile_of(t)
        gap = jnp.where(is_left, qi * TQ - (j + 1) * TK, j * TK - (qi + 1) * TQ).astype(F32)
        gap_pieces = [p.astype(F32) for p in _split3(jnp.full((8, LANES), sl * gap, F32))]
        gap_row = _lane_select(lane8, [0.0] * 9 + gap_pieces)[0:1, :].astype(BF16)
        k_ext = kx_ref[jnp.where(is_left, 0, 1)] + gap_row
        k = k_ref[pl.ds(pl.multiple_of(j * TK, TK), TK), :]

        def logits(m):
            ka = jnp.concatenate([k[:, m * DIFF_DQK:(m + 1) * DIFF_DQK], k_ext], axis=1)
            return _dot_nt(qa_ref[m], ka)

        probabilities(slot, logits)

    def diag(d, slot):
        k = k_ref[pl.ds(pl.multiple_of((ratio * qi + d) * TK, TK), TK), :]
        i_loc = lax.broadcasted_iota(jnp.int32, (TQ, TK), 0)
        j_loc = lax.broadcasted_iota(jnp.int32, (TQ, TK), 1)
        bias = sl * jnp.abs(i_loc - j_loc - d * TK).astype(F32)

        def logits(m):
            cols = slice(m * DIFF_DQK, (m + 1) * DIFF_DQK)
            return _dot_nt(q_ref[:, cols], k[:, cols]) - bias

        probabilities(slot, logits)

    def weighted_values(j, slot):
        v = v_ref[pl.ds(pl.multiple_of(j * TK, TK), TK), :]
        for m in range(2):
            acc_ref[m] += _dot(p_ref[slot, m], v)

    off_diag(0, 0)

    def pair(u, carry):
        t = 2 * u
        off_diag(t + 1, 1)
        weighted_values(tile_of(t)[1], 0)
        off_diag(t + 2, 0)
        weighted_values(tile_of(t + 1)[1], 1)
        return carry

    lax.fori_loop(0, (n_off - 2) // 2, pair, 0, unroll=7)
    off_diag(n_off - 1, 1)
    weighted_values(tile_of(n_off - 2)[1], 0)
    diag(0, 0)
    weighted_values(tile_of(n_off - 1)[1], 1)
    diag(1, 1)
    weighted_values(ratio * qi, 0)
    weighted_values(ratio * qi + 1, 1)

    _attn_finish(acc_ref, jnp.sum(l_ref[0], axis=1, keepdims=True), jnp.sum(l_ref[1], axis=1, keepdims=True),
                 lam_ref, og_ref, o_ref, lambda_init)


def _split3_np(x):
    x = np.asarray(x, np.float32)
    hi = x.astype(BF16).astype(np.float32)
    mid = (x - hi).astype(BF16).astype(np.float32)
    lo = (x - hi - mid).astype(BF16).astype(np.float32)
    return [hi, mid, lo]


def _alibi_tables(sl_np):
    nh = len(sl_np)
    ii = np.arange(TQ, dtype=np.float32)
    jj = np.arange(TK, dtype=np.float32)
    qx = np.zeros((nh, TQ, LANES), np.float32)
    kx = np.zeros((nh, 2, TK, LANES), np.float32)
    for h, s in enumerate(np.asarray(sl_np, np.float32)):
        qx[h, :, 0:3] = -1.0
        qx[h, :, 9:12] = -1.0
        for c, piece in enumerate(_split3_np(s * ii)):
            qx[h, :, 3 + c] = piece
        for c, piece in enumerate(_split3_np(s * (TQ - ii))):
            qx[h, :, 6 + c] = piece
        for c, piece in enumerate(_split3_np(s * (TK - jj))):
            kx[h, 0, :, c] = piece
        for c, piece in enumerate(_split3_np(s * jj)):
            kx[h, 1, :, c] = piece
        kx[h, 0, :, 3:6] = -1.0
        kx[h, 1, :, 6:9] = -1.0
    return jnp.asarray(qx, BF16), jnp.asarray(kx, BF16)


def _attention_bounded(qkvz3, sl_np, lam, og, lambda_init):
    nb, s, _ = qkvz3.shape
    k_col0 = D_INNER // DIFF_DV
    v_col0 = 2 * k_col0
    qx, kx = _alibi_tables(sl_np)
    return pl.pallas_call(
        functools.partial(_attn_bounded_kernel, lambda_init=lambda_init),
        grid=(nb, DIFF_HEADS, s // TQ),
        in_specs=[pl.BlockSpec(memory_space=pltpu.SMEM),
                  pl.BlockSpec((None, TQ, DIFF_DV), lambda b, h, i: (b, i, h)),
                  pl.BlockSpec((None, s, DIFF_DV), lambda b, h, i: (b, 0, k_col0 + h)),
                  pl.BlockSpec((None, s, DIFF_DV), lambda b, h, i: (b, 0, v_col0 + h)),
                  pl.BlockSpec((None, TQ, LANES), lambda b, h, i: (h, 0, 0)),
                  pl.BlockSpec((None, 2, TK, LANES), lambda b, h, i: (h, 0, 0, 0)),
                  _resident((4, DIFF_DQK)), _resident((1, DIFF_DV))],
        out_specs=pl.BlockSpec((None, TQ, DIFF_DV), lambda b, h, i: (b, i, h)),
        out_shape=jax.ShapeDtypeStruct((nb, s, D_INNER), BF16),
        scratch_shapes=[pltpu.VMEM((2, TQ, 2 * DIFF_DQK), BF16),
                        pltpu.VMEM((2, 2, TQ, TK), BF16),
                        pltpu.VMEM((2, TQ, LANES), F32),
                        pltpu.VMEM((2, TQ, DIFF_DV), F32)],
        compiler_params=_cparams(("parallel", "parallel", "arbitrary")),
        name="diff_attention_bounded",
    )(jnp.asarray(sl_np, F32), qkvz3, qkvz3, qkvz3, qx, kx, lam, og)


def _attn_online_kernel(sl_ref, q_ref, k_ref, v_ref, absd_ref, kx_ref, lam_ref, og_ref, o_ref,
                        ql_ref, qr_ref, m_ref, l_ref, acc_ref, *, lambda_init):
    TQ = TQ_ONLINE
    hd = pl.program_id(1)
    qi = pl.program_id(2)
    n_kv = k_ref.shape[0] // TQ
    slope = sl_ref[hd]

    lane = lax.broadcasted_iota(jnp.int32, (TQ, LANES), 1)
    pieces = [p.astype(F32) for p in _split3(jnp.full((TQ, LANES), slope, F32))]
    ext = _lane_select(lane, pieces + pieces).astype(BF16)
    for m in range(2):
        qm = q_ref[:, m * DIFF_DQK:(m + 1) * DIFF_DQK]
        ql_ref[m, :, :DIFF_DQK] = qm
        ql_ref[m, :, DIFF_DQK:] = ext
        qr_ref[m, :, :DIFF_DQK] = qm
        qr_ref[m, :, DIFF_DQK:] = -ext
    m_ref[...] = jnp.full(m_ref.shape, -1e30, F32)
    l_ref[...] = jnp.zeros_like(l_ref)
    acc_ref[...] = jnp.zeros_like(acc_ref)
    ii = lax.broadcasted_iota(jnp.int32, (TQ, 1), 0).astype(F32)

    def update(m, s, r, v):
        m_old = m_ref[m]
        m_new = jnp.maximum(m_old, jnp.max(s, axis=1, keepdims=True) + r)
        alpha = jnp.exp2(m_old - m_new)
        p = jnp.exp2(s - (m_new - r))
        l_ref[m] = alpha * l_ref[m] + jnp.sum(p, axis=1, keepdims=True)
        acc_ref[m] = alpha * acc_ref[m] + _dot(p.astype(BF16), v)
        m_ref[m] = m_new

    def off_diag(j, qa_ref, r):
        start = pl.multiple_of(j * TQ, TQ)
        k = k_ref[pl.ds(start, TQ), :]
        v = v_ref[pl.ds(start, TQ), :]
        for m in range(2):
            ka = jnp.concatenate([k[:, m * DIFF_DQK:(m + 1) * DIFF_DQK], kx_ref[...]], axis=1)
            update(m, _dot_nt(qa_ref[m], ka), r, v)

    def left(j, carry):
        off_diag(j, ql_ref, -slope * (((qi - j) * TQ).astype(F32) + ii))
        return carry

    def right(j, carry):
        off_diag(j, qr_ref, slope * (ii - ((j - qi) * TQ).astype(F32)))
        return carry

    lax.fori_loop(0, qi, left, 0)

    start = pl.multiple_of(qi * TQ, TQ)
    k = k_ref[pl.ds(start, TQ), :]
    v = v_ref[pl.ds(start, TQ), :]
    bias = -slope * absd_ref[...]
    zero = jnp.zeros((TQ, 1), F32)
    for m in range(2):
        cols = slice(m * DIFF_DQK, (m + 1) * DIFF_DQK)
        update(m, _dot_nt(q_ref[:, cols], k[:, cols]) + bias, zero, v)

    lax.fori_loop(qi + 1, n_kv, right, 0)

    _attn_finish(acc_ref, l_ref[0], l_ref[1], lam_ref, og_ref, o_ref, lambda_init)


def _attention_online(qkvz3, sl, lam, og, lambda_init):
    TQ = TQ_ONLINE
    nb, s, _ = qkvz3.shape
    k_col0 = D_INNER // DIFF_DV
    v_col0 = 2 * k_col0
    pos = np.arange(TQ)
    absd = jnp.asarray(np.abs(pos[:, None] - pos[None, :]), F32)
    kx_np = np.zeros((TQ, LANES), np.float32)
    kx_np[:, 0:3] = ((pos // 16) * 16)[:, None]
    kx_np[:, 3:6] = (pos % 16)[:, None]
    kx = jnp.asarray(kx_np, BF16)
    return pl.pallas_call(
        functools.partial(_attn_online_kernel, lambda_init=lambda_init),
        grid=(nb, DIFF_HEADS, s // TQ),
        in_specs=[pl.BlockSpec(memory_space=pltpu.SMEM),
                  pl.BlockSpec((None, TQ, DIFF_DV), lambda b, h, i: (b, i, h)),
                  pl.BlockSpec((None, s, DIFF_DV), lambda b, h, i: (b, 0, k_col0 + h)),
                  pl.BlockSpec((None, s, DIFF_DV), lambda b, h, i: (b, 0, v_col0 + h)),
                  _resident((TQ, TQ)), _resident((TQ, LANES)),
                  _resident((4, DIFF_DQK)), _resident((1, DIFF_DV))],
        out_specs=pl.BlockSpec((None, TQ, DIFF_DV), lambda b, h, i: (b, i, h)),
        out_shape=jax.ShapeDtypeStruct((nb, s, D_INNER), BF16),
        scratch_shapes=[pltpu.VMEM((2, TQ, 2 * DIFF_DQK), BF16), pltpu.VMEM((2, TQ, 2 * DIFF_DQK), BF16),
                        pltpu.VMEM((2, TQ, 1), F32), pltpu.VMEM((2, TQ, 1), F32),
                        pltpu.VMEM((2, TQ, DIFF_DV), F32)],
        compiler_params=_cparams(("parallel", "parallel", "arbitrary")),
        name="diff_attention_online",
    )(sl, qkvz3, qkvz3, qkvz3, absd, kx, lam, og)


def _prep_a(g, w_in, v_g, w_s, b_s, w_out):
    return (g[None], w_in.astype(BF16), v_g[None], w_s.astype(BF16), b_s[:, :, None], w_out.astype(BF16))


def _prep_b(g, w_in, w_gate, gate_bias, o_g, w_out):
    c_q, c_k, c_v, c_g, c_a = 0, GLA_KEY, 2 * GLA_KEY, 2 * GLA_KEY + D_INNER, 2 * GLA_KEY + 2 * D_INNER
    w_main = jnp.concatenate([w_in[:, c_v:c_g], w_in[:, c_g:c_a], w_in[:, c_q:c_k], w_in[:, c_k:c_v]],
                             axis=1).astype(BF16)
    w_a = jnp.pad(w_in[:, c_a:], ((0, 0), (0, LANES - 2 * GLA_RANK))).astype(BF16)
    wg = jnp.zeros((LANES, 2 * GLA_KEY), F32)
    wg = wg.at[:GLA_RANK, :GLA_KEY].set(w_gate[0]).at[GLA_RANK:2 * GLA_RANK, GLA_KEY:].set(w_gate[1])
    rows = np.arange(TM)
    same_chunk = (rows[:, None] // GLA_CHUNK) == (rows[None, :] // GLA_CHUNK)
    tril = jnp.asarray(same_chunk & (rows[None, :] <= rows[:, None]), BF16)
    triu = jnp.asarray(same_chunk & (rows[None, :] >= rows[:, None]), BF16)
    return (g[None], w_main, w_a, wg.astype(BF16), gate_bias.reshape(1, -1), tril, triu, o_g[None],
            w_out.astype(BF16))


def _layer_b(x, g, w_main, w_a, wg, gate_bias, tril, triu, o_g, w_out):
    nb = x.shape[0] // SEQ
    main, cf, cb = _inproj_b(x, g, w_main, w_a, wg, gate_bias, tril, triu)
    o_f, o_b = _gla(main.reshape(nb, SEQ, -1), cf.reshape(nb, SEQ, -1), cb.reshape(nb, SEQ, -1))
    return _outproj_b(o_f.reshape(-1, D_INNER), o_b.reshape(-1, D_INNER), main, o_g, w_out, x)


def _prep_c(g, w_in, q_g, k_g, lam, o_g, w_out, lambda_init):
    logit_bound = jnp.max(jnp.abs(q_g)) * jnp.max(jnp.abs(k_g)) * (DIFF_DQK ** 0.5 * LOG2E * 1.02)
    return (g[None], w_in.astype(BF16), q_g[None], k_g[None], lam, o_g[None], w_out.astype(BF16),
            logit_bound <= MAX_UNSHIFTED_LOG2, lambda_init)


def _layer_c(x, g, w_in, q_g, k_g, lam, o_g, w_out, logits_bounded, lambda_init):
    nb = x.shape[0] // SEQ
    qkvz = _inproj_c(x, g, w_in, q_g, k_g)
    sl_np = (2.0 ** (-8.0 * np.arange(1, DIFF_HEADS + 1) / DIFF_HEADS) * LOG2E).astype(np.float32)
    qkvz3 = qkvz.reshape(nb, SEQ, -1)
    o = lax.cond(logits_bounded,
                 lambda: _attention_bounded(qkvz3, sl_np, lam, o_g, lambda_init),
                 lambda: _attention_online(qkvz3, jnp.asarray(sl_np), lam, o_g, lambda_init))
    return _outproj_c(o.reshape(-1, D_INNER), qkvz, w_out, x)


def kernel(x_prompt, x_sample, norm_g, a_w_in, a_v_g, a_w_s, a_b_s, a_w_out, b_w_in, b_w_gate, b_gate_bias,
           b_o_g, b_w_out, c_w_in, c_q_g, c_k_g, c_lam, c_o_g, c_w_out):
    layers = []
    for i in range(DEPTH):
        kind, j = i % 3, i // 3
        if kind == 0:
            layers.append((_layer_a_call, _prep_a(norm_g[i], a_w_in[j], a_v_g[j], a_w_s[j], a_b_s[j], a_w_out[j])))
        elif kind == 1:
            layers.append((_layer_b, _prep_b(norm_g[i], b_w_in[j], b_w_gate[j], b_gate_bias[j], b_o_g[j],
                                             b_w_out[j])))
        else:
            lambda_init = 0.8 - 0.6 * math.exp(-0.3 * i)
            layers.append((_layer_c, _prep_c(norm_g[i], c_w_in[j], c_q_g[j], c_k_g[j], c_lam[j], c_o_g[j],
                                             c_w_out[j], lambda_init)))

    def trunk(x_group):
        x = x_group.reshape(-1, D_MODEL)
        for layer, params in layers:
            x = layer(x, *params)
        return x.reshape(x_group.shape)

    return trunk(x_prompt), trunk(x_sample)
```

```python
import functools
import math

import jax
import jax.numpy as jnp
import numpy as np
from jax import lax
from jax.experimental import pallas as pl
from jax.experimental.pallas import tpu as pltpu

F32 = jnp.float32
BF16 = jnp.bfloat16

D_MODEL = 1024
SEQ = 16384
DEPTH = 4
D_INNER = 2048
EPS = 1e-6

SG_CHUNK = 128
SG_GROUPS = 8
SG_GDIM = D_INNER // SG_GROUPS

GLA_HEADS = 4
GLA_KEY = 512
GLA_DK = 128
GLA_DV = 512
GLA_RANK = 16
GLA_TAU = 16.0
GLA_CHUNK = 64

DIFF_HEADS = 8
DIFF_DQK = 128
DIFF_DV = 256

LANES = 128
MXU_DEPTH = 256
VMEM_LIMIT = 56 * 1024 * 1024

TM = 512
GLA_ROWS = 256
TQ = 1024
TK = 512
TQ_ONLINE = 512
LOG2E = math.log2(math.e)
MAX_UNSHIFTED_LOG2 = 80.0


def _cparams(sem):
    return pltpu.CompilerParams(dimension_semantics=sem, vmem_limit_bytes=VMEM_LIMIT)


def _resident(shape):
    nd = len(shape)
    return pl.BlockSpec(shape, lambda *_: (0,) * nd, pipeline_mode=pl.Buffered(1))


def _rms(x, g):
    return x * lax.rsqrt(jnp.mean(x * x, axis=-1, keepdims=True) + EPS) * g


def _silu(z):
    return z * jax.nn.sigmoid(z)


def _dot(a, b):
    return jnp.dot(a, b, preferred_element_type=F32)


def _dot_nt(a, b):
    return lax.dot_general(a, b, (((1,), (1,)), ((), ())), preferred_element_type=F32)


def _dot_tn(a, b):
    return lax.dot_general(a, b, (((0,), (0,)), ((), ())), preferred_element_type=F32)


def _layer_a_kernel(x_ref, g_ref, w_in_ref, vg_ref, ws_ref, bs_ref, w_out_ref, o_ref, uvz_ref, y_ref):
    x = x_ref[...]
    h = _rms(x, g_ref[...]).astype(BF16)
    for blk in range(3):
        lo = blk * D_INNER
        y = _dot(h, w_in_ref[:, lo:lo + D_INNER])
        if blk == 1:
            y = _rms(y, vg_ref[...])
        uvz_ref[:, lo:lo + D_INNER] = y.astype(BF16)
    for c in range(TM // SG_CHUNK):
        rows = slice(c * SG_CHUNK, (c + 1) * SG_CHUNK)
        for grp in range(SG_GROUPS):
            u_cols = slice(grp * SG_GDIM, (grp + 1) * SG_GDIM)
            v_cols = slice(D_INNER + u_cols.start, D_INNER + u_cols.stop)
            z_cols = slice(2 * D_INNER + u_cols.start, 2 * D_INNER + u_cols.stop)
            sv = _dot(ws_ref[grp], uvz_ref[rows, v_cols]) + bs_ref[grp]
            y = uvz_ref[rows, u_cols].astype(F32) * sv * _silu(uvz_ref[rows, z_cols].astype(F32))
            y_ref[rows, u_cols] = y.astype(BF16)
    o_ref[...] = x + _dot(y_ref[...], w_out_ref[...])


def _layer_a_call(x, g, w_in, vg, ws, bs, w_out):
    t = x.shape[0]
    n = 3 * D_INNER
    return pl.pallas_call(
        _layer_a_kernel,
        grid=(t // TM,),
        in_specs=[pl.BlockSpec((TM, D_MODEL), lambda i: (i, 0)),
                  _resident((1, D_MODEL)), _resident((D_MODEL, n)), _resident((1, D_INNER)),
                  _resident((SG_GROUPS, SG_CHUNK, SG_CHUNK)), _resident((SG_GROUPS, SG_CHUNK, 1)),
                  _resident((D_INNER, D_MODEL))],
        out_specs=pl.BlockSpec((TM, D_MODEL), lambda i: (i, 0)),
        out_shape=jax.ShapeDtypeStruct((t, D_MODEL), F32),
        scratch_shapes=[pltpu.VMEM((TM, n), BF16), pltpu.VMEM((TM, D_INNER), BF16)],
        compiler_params=_cparams(("parallel",)),
        name="layer_a",
    )(x, g, w_in, vg, ws, bs, w_out)


def _log_sigmoid(x):
    return -(jnp.maximum(-x, 0.0) + jnp.log1p(jnp.exp(-jnp.abs(x))))


def _split3(x):
    hi = x.astype(BF16)
    r1 = x - hi.astype(F32)
    mid = r1.astype(BF16)
    lo = (r1 - mid.astype(F32)).astype(BF16)
    return hi, mid, lo


def _inproj_b_kernel(x_ref, g_ref, w_ref, wa_ref, wg_ref, gb_ref, tril_ref, triu_ref,
                     o_ref, cf_ref, cb_ref):
    h = _rms(x_ref[...], g_ref[...]).astype(BF16)
    code = _dot(h, wa_ref[...]).astype(BF16)
    la = _log_sigmoid(_dot(code, wg_ref[...]) + gb_ref[...]) / GLA_TAU
    pieces_f = _split3(la[:, :GLA_KEY])[:2]
    pieces_b = _split3(la[:, GLA_KEY:])[:2]
    for blk in range(2):
        lo = blk * D_INNER
        o_ref[:, lo:lo + D_INNER] = _dot(h, w_ref[:, lo:lo + D_INNER]).astype(BF16)
    lo = 2 * D_INNER
    qk = _dot(h, w_ref[:, lo:lo + 2 * GLA_KEY])
    o_ref[:, lo:lo + GLA_KEY] = (qk[:, :GLA_KEY] * GLA_DK ** -0.5).astype(BF16)
    o_ref[:, lo + GLA_KEY:lo + 2 * GLA_KEY] = qk[:, GLA_KEY:].astype(BF16)
    for blk in range(TM // MXU_DEPTH):
        rows = slice(blk * MXU_DEPTH, (blk + 1) * MXU_DEPTH)
        cf_ref[rows, :] = sum(_dot(tril_ref[...], piece[rows, :]) for piece in pieces_f)
        cb_ref[rows, :] = sum(_dot(triu_ref[...], piece[rows, :]) for piece in pieces_b)


def _inproj_b(x, g, w_main, w_a, w_gate, gate_bias, tril, triu):
    t = x.shape[0]
    n = w_main.shape[1]
    return pl.pallas_call(
        _inproj_b_kernel,
        grid=(t // TM,),
        in_specs=[pl.BlockSpec((TM, D_MODEL), lambda i: (i, 0)),
                  _resident((1, D_MODEL)), _resident((D_MODEL, n)), _resident((D_MODEL, LANES)),
                  _resident((LANES, 2 * GLA_KEY)), _resident((1, 2 * GLA_KEY)),
                  _resident((MXU_DEPTH, MXU_DEPTH)), _resident((MXU_DEPTH, MXU_DEPTH))],
        out_specs=[pl.BlockSpec((TM, n), lambda i: (i, 0)),
                   pl.BlockSpec((TM, GLA_KEY), lambda i: (i, 0)),
                   pl.BlockSpec((TM, GLA_KEY), lambda i: (i, 0))],
        out_shape=[jax.ShapeDtypeStruct((t, n), BF16),
                   jax.ShapeDtypeStruct((t, GLA_KEY), F32),
                   jax.ShapeDtypeStruct((t, GLA_KEY), F32)],
        compiler_params=_cparams(("parallel",)),
        name="inproj_b",
    )(x, g, w_main, w_a, w_gate, gate_bias, tril, triu)


def _inproj_c_kernel(x_ref, g_ref, w_ref, qg_ref, kg_ref, o_ref):
    h = _rms(x_ref[...], g_ref[...]).astype(BF16)
    half = D_INNER // 2
    for blk in range(8):
        lo = blk * half
        y = _dot(h, w_ref[:, lo:lo + half])
        if blk < 4:
            gain = qg_ref[...] if blk < 2 else kg_ref[...]
            scale = DIFF_DQK ** -0.5 * LOG2E if blk < 2 else 1.0
            for grp in range(half // DIFF_DQK):
                sl = slice(grp * DIFF_DQK, (grp + 1) * DIFF_DQK)
                o_ref[:, lo + sl.start:lo + sl.stop] = (_rms(y[:, sl], gain) * scale).astype(BF16)
        else:
            o_ref[:, lo:lo + half] = y.astype(BF16)


def _inproj_c(x, g, w, qg, kg):
    t = x.shape[0]
    n = 4 * D_INNER
    return pl.pallas_call(
        _inproj_c_kernel,
        grid=(t // TM,),
        in_specs=[pl.BlockSpec((TM, D_MODEL), lambda i: (i, 0)),
                  _resident((1, D_MODEL)), _resident((D_MODEL, n)),
                  _resident((1, DIFF_DQK)), _resident((1, DIFF_DQK))],
        out_specs=pl.BlockSpec((TM, n), lambda i: (i, 0)),
        out_shape=jax.ShapeDtypeStruct((t, n), BF16),
        compiler_params=_cparams(("parallel",)),
        name="inproj_c",
    )(x, g, w, qg, kg)


def _outproj_b_kernel(of_ref, ob_ref, g_ref, og_ref, w_ref, x_ref, o_ref):
    out = x_ref[...]
    for hd in range(GLA_HEADS):
        cols = slice(hd * GLA_DV, (hd + 1) * GLA_DV)
        o = of_ref[:, cols].astype(F32) + ob_ref[:, cols].astype(F32)
        y = _rms(o, og_ref[...]) * _silu(g_ref[:, cols].astype(F32))
        out = out + _dot(y.astype(BF16), w_ref[cols, :])
    o_ref[...] = out


def _outproj_b(o_f, o_b, main, og, w, x):
    t = x.shape[0]
    return pl.pallas_call(
        _outproj_b_kernel,
        grid=(t // TM,),
        in_specs=[pl.BlockSpec((TM, D_INNER), lambda i: (i, 0)),
                  pl.BlockSpec((TM, D_INNER), lambda i: (i, 0)),
                  pl.BlockSpec((TM, D_INNER), lambda i: (i, 1)),
                  _resident((1, GLA_DV)), _resident((D_INNER, D_MODEL)),
                  pl.BlockSpec((TM, D_MODEL), lambda i: (i, 0))],
        out_specs=pl.BlockSpec((TM, D_MODEL), lambda i: (i, 0)),
        out_shape=jax.ShapeDtypeStruct((t, D_MODEL), F32),
        compiler_params=_cparams(("parallel",)),
        name="outproj_b",
    )(o_f, o_b, main, og, w, x)


def _outproj_c_kernel(a_ref, z_ref, w_ref, x_ref, o_ref):
    out = x_ref[...]
    quarter = D_INNER // 4
    for part in range(4):
        cols = slice(part * quarter, (part + 1) * quarter)
        y = (a_ref[:, cols].astype(F32) * _silu(z_ref[:, cols].astype(F32))).astype(BF16)
        out = out + _dot(y, w_ref[cols, :])
    o_ref[...] = out


def _outproj_c(o, qkvz, w, x):
    t = x.shape[0]
    return pl.pallas_call(
        _outproj_c_kernel,
        grid=(t // TM,),
        in_specs=[pl.BlockSpec((TM, D_INNER), lambda i: (i, 0)),
                  pl.BlockSpec((TM, D_INNER), lambda i: (i, 3)),
                  _resident((D_INNER, D_MODEL)),
                  pl.BlockSpec((TM, D_MODEL), lambda i: (i, 0))],
        out_specs=pl.BlockSpec((TM, D_MODEL), lambda i: (i, 0)),
        out_shape=jax.ShapeDtypeStruct((t, D_MODEL), F32),
        compiler_params=_cparams(("parallel",)),
        name="outproj_c",
    )(o, qkvz, w, x)


def _gla_pair(q_ref, k_ref, c_ref, ct_ref, v_ref, o_ref, s_ref, hd, pair, reverse):
    i_ref, i_last = (GLA_CHUNK // 2 - 1, 0) if reverse else (GLA_CHUNK // 2, GLA_CHUNK - 1)
    kcols = slice(hd * GLA_DK, (hd + 1) * GLA_DK)
    vcols = slice(hd * GLA_DV, (hd + 1) * GLA_DV)
    r0 = pair * 2 * GLA_CHUNK
    first, second = slice(r0, r0 + GLA_CHUNK), slice(r0 + GLA_CHUNK, r0 + 2 * GLA_CHUNK)
    rows_a, rows_b = (second, first) if reverse else (first, second)

    def in_memory_order(of_a, of_b):
        return jnp.concatenate([of_b, of_a] if reverse else [of_a, of_b], axis=0)

    def prepare(rows):
        q = q_ref[rows, kcols].astype(F32)
        k = k_ref[rows, kcols].astype(F32)
        cum = c_ref[rows, kcols]
        ref = cum[i_ref:i_ref + 1, :]
        last = cum[i_last:i_last + 1, :]
        return (q * jnp.exp(cum - ref), k * jnp.exp(ref - cum), q * jnp.exp(cum), k * jnp.exp(last - cum),
                jnp.exp(last))

    qg_a, kg_a, qi_a, ki_a, d_a = prepare(rows_a)
    qg_b, kg_b, qi_b, ki_b, d_b = prepare(rows_b)
    v_pair = v_ref[r0:r0 + 2 * GLA_CHUNK, vcols]
    zeros = jnp.zeros((GLA_CHUNK, GLA_DK), BF16)

    yield
    keys_for_a = in_memory_order(kg_a.astype(BF16), kg_b.astype(BF16))
    w_a = _dot_nt(qg_a.astype(BF16), keys_for_a)
    keys_for_b = in_memory_order(jnp.concatenate([zeros, ki_a.astype(BF16)], axis=1),
                                 jnp.concatenate([kg_b.astype(BF16), zeros], axis=1))
    w_b = _dot_nt(jnp.concatenate([qg_b, qi_b], axis=1).astype(BF16), keys_for_b)
    row = lax.broadcasted_iota(jnp.int32, (GLA_CHUNK, 2 * GLA_CHUNK), 0)
    col = lax.broadcasted_iota(jnp.int32, (GLA_CHUNK, 2 * GLA_CHUNK), 1)
    if reverse:
        keep_a = (col >= GLA_CHUNK) & (row < col - GLA_CHUNK)
        keep_b = (col >= GLA_CHUNK) | (row < col)
    else:
        keep_a = (col < GLA_CHUNK) & (row >= col)
        keep_b = (col < GLA_CHUNK) | (row >= col - GLA_CHUNK)
    w_a = jnp.where(keep_a, w_a, 0.0).astype(BF16)
    w_b = jnp.where(keep_b, w_b, 0.0).astype(BF16)

    lhs = jnp.concatenate([in_memory_order(qi_a.astype(BF16), (qi_b * d_a).astype(BF16)),
                           in_memory_order(w_a, w_b)], axis=1)
    last_a = rows_a.start + i_last
    last_b = rows_b.start + i_last
    decay = jnp.exp(ct_ref[kcols, last_a:last_a + 1] + ct_ref[kcols, last_b:last_b + 1])
    keys_to_state = in_memory_order((ki_a * d_b).astype(BF16), ki_b.astype(BF16))

    yield
    state = s_ref[hd]
    rhs = jnp.concatenate([state.astype(BF16), v_pair], axis=0)
    o_ref[r0:r0 + 2 * GLA_CHUNK, vcols] = _dot(lhs, rhs).astype(o_ref.dtype)
    s_ref[hd] = state * decay + _dot_tn(keys_to_state, v_pair)
    yield


def _gla_kernel(qf_ref, kf_ref, cf_ref, vf_ref, qb_ref, kb_ref, cb_ref, vb_ref,
                of_ref, ob_ref, sf_ref, sb_ref, ctf_ref, ctb_ref):
    @pl.when(pl.program_id(1) == 0)
    def _():
        sf_ref[...] = jnp.zeros_like(sf_ref)
        sb_ref[...] = jnp.zeros_like(sb_ref)

    ctf_ref[...] = cf_ref[...].T
    ctb_ref[...] = cb_ref[...].T
    n_pairs = GLA_ROWS // (2 * GLA_CHUNK)
    pairs = []
    for p in range(n_pairs):
        for hd in range(GLA_HEADS):
            pairs.append(_gla_pair(qf_ref, kf_ref, cf_ref, ctf_ref, vf_ref, of_ref, sf_ref, hd, p, reverse=False))
            pairs.append(_gla_pair(qb_ref, kb_ref, cb_ref, ctb_ref, vb_ref, ob_ref, sb_ref, hd, n_pairs - 1 - p,
                                   reverse=True))
    for _stage in range(3):
        for pair in pairs:
            next(pair)


def _gla(main3, cf3, cb3):
    nb, s, _ = main3.shape
    steps = s // GLA_ROWS
    q_col = 2 * D_INNER // GLA_KEY
    k_col = q_col + 1

    def fwd(col):
        return lambda b, i: (b, i, col)

    def bwd(col):
        return lambda b, i: (b, steps - 1 - i, col)

    def specs(idx):
        return [pl.BlockSpec((None, GLA_ROWS, GLA_KEY), idx(q_col)),
                pl.BlockSpec((None, GLA_ROWS, GLA_KEY), idx(k_col)),
                pl.BlockSpec((None, GLA_ROWS, GLA_KEY), idx(0)),
                pl.BlockSpec((None, GLA_ROWS, D_INNER), idx(0))]

    out_sd = jax.ShapeDtypeStruct((nb, s, D_INNER), BF16)
    state = pltpu.VMEM((GLA_HEADS, GLA_DK, GLA_DV), F32)
    cum_t = pltpu.VMEM((GLA_KEY, GLA_ROWS), F32)
    return pl.pallas_call(
        _gla_kernel,
        grid=(nb, steps),
        in_specs=specs(fwd) + specs(bwd),
        out_specs=[pl.BlockSpec((None, GLA_ROWS, D_INNER), fwd(0)),
                   pl.BlockSpec((None, GLA_ROWS, D_INNER), bwd(0))],
        out_shape=[out_sd, out_sd],
        scratch_shapes=[state, state, cum_t, cum_t],
        compiler_params=_cparams(("parallel", "arbitrary")),
        name="gla_scan",
    )(main3, main3, cf3, main3, main3, main3, cb3, main3)


def _attn_finish(acc_ref, l0, l1, lam_ref, og_ref, o_ref, lambda_init):
    lam = lam_ref[...]
    lam_full = (jnp.exp(jnp.sum(lam[0:1] * lam[1:2], axis=1, keepdims=True))
                - jnp.exp(jnp.sum(lam[2:3] * lam[3:4], axis=1, keepdims=True)) + lambda_init)
    o = acc_ref[0] * (1.0 / l0) - lam_full * (acc_ref[1] * (1.0 / l1))
    o_ref[...] = (_rms(o, og_ref[...]) * (1.0 - lambda_init)).astype(o_ref.dtype)


def _lane_select(lane, pieces):
    out = 0.0
    for c, piece in reversed(list(enumerate(pieces))):
        out = jnp.where(lane == c, piece, out)
    return out


def _attn_bounded_kernel(sl_ref, q_ref, k_ref, v_ref, qx_ref, kx_ref, lam_ref, og_ref, o_ref,
                         qa_ref, p_ref, l_ref, acc_ref, *, lambda_init):
    hd = pl.program_id(1)
    qi = pl.program_id(2)
    ratio = TQ // TK
    n_off = k_ref.shape[0] // TK - ratio
    assert ratio == 2 and n_off >= 2 and n_off % 2 == 0
    sl = sl_ref[hd]

    for m in range(2):
        qa_ref[m, :, :DIFF_DQK] = q_ref[:, m * DIFF_DQK:(m + 1) * DIFF_DQK]
        qa_ref[m, :, DIFF_DQK:] = qx_ref[...]
    l_ref[...] = jnp.zeros_like(l_ref)
    acc_ref[...] = jnp.zeros_like(acc_ref)
    lane8 = lax.broadcasted_iota(jnp.int32, (8, LANES), 1)

    def tile_of(t):
        is_left = t < ratio * qi
        return is_left, jnp.where(is_left, t, t + ratio)

    def probabilities(slot, logits_of_map):
        for m in range(2):
            p = jnp.exp2(logits_of_map(m))
            part = p[:, :LANES]
            for c in range(1, TK // LANES):
                part = part + p[:, c * LANES:(c + 1) * LANES]
            l_ref[m] += part
            p_ref[slot, m] = p.astype(BF16)

    def off_diag(t, slot):
        is_left, j = tile_of(t)
        gap = jnp.where(is_left, qi * TQ - (j + 1) * TK, j * TK - (qi + 1) * TQ).astype(F32)
        gap_pieces = [p.astype(F32) for p in _split3(jnp.full((8, LANES), sl * gap, F32))]
        gap_row = _lane_select(lane8, [0.0] * 9 + gap_pieces)[0:1, :].astype(BF16)
        k_ext = kx_ref[jnp.where(is_left, 0, 1)] + gap_row
        k = k_ref[pl.ds(pl.multiple_of(j * TK, TK), TK), :]

        def logits(m):
            ka = jnp.concatenate([k[:, m * DIFF_DQK:(m + 1) * DIFF_DQK], k_ext], axis=1)
            return _dot_nt(qa_ref[m], ka)

        probabilities(slot, logits)

    def diag(d, slot):
        k = k_ref[pl.ds(pl.multiple_of((ratio * qi + d) * TK, TK), TK), :]
        i_loc = lax.broadcasted_iota(jnp.int32, (TQ, TK), 0)
        j_loc = lax.broadcasted_iota(jnp.int32, (TQ, TK), 1)
        bias = sl * jnp.abs(i_loc - j_loc - d * TK).astype(F32)

        def logits(m):
            cols = slice(m * DIFF_DQK, (m + 1) * DIFF_DQK)
            return _dot_nt(q_ref[:, cols], k[:, cols]) - bias

        probabilities(slot, logits)

    def weighted_values(j, slot):
        v = v_ref[pl.ds(pl.multiple_of(j * TK, TK), TK), :]
        for m in range(2):
            acc_ref[m] += _dot(p_ref[slot, m], v)

    off_diag(0, 0)

    def pair(u, carry):
        t = 2 * u
        off_diag(t + 1, 1)
        weighted_values(tile_of(t)[1], 0)
        off_diag(t + 2, 0)
        weighted_values(tile_of(t + 1)[1], 1)
        return carry

    lax.fori_loop(0, (n_off - 2) // 2, pair, 0, unroll=7)
    off_diag(n_off - 1, 1)
    weighted_values(tile_of(n_off - 2)[1], 0)
    diag(0, 0)
    weighted_values(tile_of(n_off - 1)[1], 1)
    diag(1, 1)
    weighted_values(ratio * qi, 0)
    weighted_values(ratio * qi + 1, 1)

    _attn_finish(acc_ref, jnp.sum(l_ref[0], axis=1, keepdims=True), jnp.sum(l_ref[1], axis=1, keepdims=True),
                 lam_ref, og_ref, o_ref, lambda_init)


def _split3_np(x):
    x = np.asarray(x, np.float32)
    hi = x.astype(BF16).astype(np.float32)
    mid = (x - hi).astype(BF16).astype(np.float32)
    lo = (x - hi - mid).astype(BF16).astype(np.float32)
    return [hi, mid, lo]


def _alibi_tables(sl_np):
    nh = len(sl_np)
    ii = np.arange(TQ, dtype=np.float32)
    jj = np.arange(TK, dtype=np.float32)
    qx = np.zeros((nh, TQ, LANES), np.float32)
    kx = np.zeros((nh, 2, TK, LANES), np.float32)
    for h, s in enumerate(np.asarray(sl_np, np.float32)):
        qx[h, :, 0:3] = -1.0
        qx[h, :, 9:12] = -1.0
        for c, piece in enumerate(_split3_np(s * ii)):
            qx[h, :, 3 + c] = piece
        for c, piece in enumerate(_split3_np(s * (TQ - ii))):
            qx[h, :, 6 + c] = piece
        for c, piece in enumerate(_split3_np(s * (TK - jj))):
            kx[h, 0, :, c] = piece
        for c, piece in enumerate(_split3_np(s * jj)):
            kx[h, 1, :, c] = piece
        kx[h, 0, :, 3:6] = -1.0
        kx[h, 1, :, 6:9] = -1.0
    return jnp.asarray(qx, BF16), jnp.asarray(kx, BF16)


def _attention_bounded(qkvz3, sl_np, lam, og, lambda_init):
    nb, s, _ = qkvz3.shape
    k_col0 = D_INNER // DIFF_DV
    v_col0 = 2 * k_col0
    qx, kx = _alibi_tables(sl_np)
    return pl.pallas_call(
        functools.partial(_attn_bounded_kernel, lambda_init=lambda_init),
        grid=(nb, DIFF_HEADS, s // TQ),
        in_specs=[pl.BlockSpec(memory_space=pltpu.SMEM),
                  pl.BlockSpec((None, TQ, DIFF_DV), lambda b, h, i: (b, i, h)),
                  pl.BlockSpec((None, s, DIFF_DV), lambda b, h, i: (b, 0, k_col0 + h)),
                  pl.BlockSpec((None, s, DIFF_DV), lambda b, h, i: (b, 0, v_col0 + h)),
                  pl.BlockSpec((None, TQ, LANES), lambda b, h, i: (h, 0, 0)),
                  pl.BlockSpec((None, 2, TK, LANES), lambda b, h, i: (h, 0, 0, 0)),
                  _resident((4, DIFF_DQK)), _resident((1, DIFF_DV))],
        out_specs=pl.BlockSpec((None, TQ, DIFF_DV), lambda b, h, i: (b, i, h)),
        out_shape=jax.ShapeDtypeStruct((nb, s, D_INNER), BF16),
        scratch_shapes=[pltpu.VMEM((2, TQ, 2 * DIFF_DQK), BF16),
                        pltpu.VMEM((2, 2, TQ, TK), BF16),
                        pltpu.VMEM((2, TQ, LANES), F32),
                        pltpu.VMEM((2, TQ, DIFF_DV), F32)],
        compiler_params=_cparams(("parallel", "parallel", "arbitrary")),
        name="diff_attention_bounded",
    )(jnp.asarray(sl_np, F32), qkvz3, qkvz3, qkvz3, qx, kx, lam, og)


def _attn_online_kernel(sl_ref, q_ref, k_ref, v_ref, absd_ref, kx_ref, lam_ref, og_ref, o_ref,
                        ql_ref, qr_ref, m_ref, l_ref, acc_ref, *, lambda_init):
    TQ = TQ_ONLINE
    hd = pl.program_id(1)
    qi = pl.program_id(2)
    n_kv = k_ref.shape[0] // TQ
    slope = sl_ref[hd]

    lane = lax.broadcasted_iota(jnp.int32, (TQ, LANES), 1)
    pieces = [p.astype(F32) for p in _split3(jnp.full((TQ, LANES), slope, F32))]
    ext = _lane_select(lane, pieces + pieces).astype(BF16)
    for m in range(2):
        qm = q_ref[:, m * DIFF_DQK:(m + 1) * DIFF_DQK]
        ql_ref[m, :, :DIFF_DQK] = qm
        ql_ref[m, :, DIFF_DQK:] = ext
        qr_ref[m, :, :DIFF_DQK] = qm
        qr_ref[m, :, DIFF_DQK:] = -ext
    m_ref[...] = jnp.full(m_ref.shape, -1e30, F32)
    l_ref[...] = jnp.zeros_like(l_ref)
    acc_ref[...] = jnp.zeros_like(acc_ref)
    ii = lax.broadcasted_iota(jnp.int32, (TQ, 1), 0).astype(F32)

    def update(m, s, r, v):
        m_old = m_ref[m]
        m_new = jnp.maximum(m_old, jnp.max(s, axis=1, keepdims=True) + r)
        alpha = jnp.exp2(m_old - m_new)
        p = jnp.exp2(s - (m_new - r))
        l_ref[m] = alpha * l_ref[m] + jnp.sum(p, axis=1, keepdims=True)
        acc_ref[m] = alpha * acc_ref[m] + _dot(p.astype(BF16), v)
        m_ref[m] = m_new

    def off_diag(j, qa_ref, r):
        start = pl.multiple_of(j * TQ, TQ)
        k = k_ref[pl.ds(start, TQ), :]
        v = v_ref[pl.ds(start, TQ), :]
        for m in range(2):
            ka = jnp.concatenate([k[:, m * DIFF_DQK:(m + 1) * DIFF_DQK], kx_ref[...]], axis=1)
            update(m, _dot_nt(qa_ref[m], ka), r, v)

    def left(j, carry):
        off_diag(j, ql_ref, -slope * (((qi - j) * TQ).astype(F32) + ii))
        return carry

    def right(j, carry):
        off_diag(j, qr_ref, slope * (ii - ((j - qi) * TQ).astype(F32)))
        return carry

    lax.fori_loop(0, qi, left, 0)

    start = pl.multiple_of(qi * TQ, TQ)
    k = k_ref[pl.ds(start, TQ), :]
    v = v_ref[pl.ds(start, TQ), :]
    bias = -slope * absd_ref[...]
    zero = jnp.zeros((TQ, 1), F32)
    for m in range(2):
        cols = slice(m * DIFF_DQK, (m + 1) * DIFF_DQK)
        update(m, _dot_nt(q_ref[:, cols], k[:, cols]) + bias, zero, v)

    lax.fori_loop(qi + 1, n_kv, right, 0)

    _attn_finish(acc_ref, l_ref[0], l_ref[1], lam_ref, og_ref, o_ref, lambda_init)


def _attention_online(qkvz3, sl, lam, og, lambda_init):
    TQ = TQ_ONLINE
    nb, s, _ = qkvz3.shape
    k_col0 = D_INNER // DIFF_DV
    v_col0 = 2 * k_col0
    pos = np.arange(TQ)
    absd = jnp.asarray(np.abs(pos[:, None] - pos[None, :]), F32)
    kx_np = np.zeros((TQ, LANES), np.float32)
    kx_np[:, 0:3] = ((pos // 16) * 16)[:, None]
    kx_np[:, 3:6] = (pos % 16)[:, None]
    kx = jnp.asarray(kx_np, BF16)
    return pl.pallas_call(
        functools.partial(_attn_online_kernel, lambda_init=lambda_init),
        grid=(nb, DIFF_HEADS, s // TQ),
        in_specs=[pl.BlockSpec(memory_space=pltpu.SMEM),
                  pl.BlockSpec((None, TQ, DIFF_DV), lambda b, h, i: (b, i, h)),
                  pl.BlockSpec((None, s, DIFF_DV), lambda b, h, i: (b, 0, k_col0 + h)),
                  pl.BlockSpec((None, s, DIFF_DV), lambda b, h, i: (b, 0, v_col0 + h)),
                  _resident((TQ, TQ)), _resident((TQ, LANES)),
                  _resident((4, DIFF_DQK)), _resident((1, DIFF_DV))],
        out_specs=pl.BlockSpec((None, TQ, DIFF_DV), lambda b, h, i: (b, i, h)),
        out_shape=jax.ShapeDtypeStruct((nb, s, D_INNER), BF16),
        scratch_shapes=[pltpu.VMEM((2, TQ, 2 * DIFF_DQK), BF16), pltpu.VMEM((2, TQ, 2 * DIFF_DQK), BF16),
                        pltpu.VMEM((2, TQ, 1), F32), pltpu.VMEM((2, TQ, 1), F32),
                        pltpu.VMEM((2, TQ, DIFF_DV), F32)],
        compiler_params=_cparams(("parallel", "parallel", "arbitrary")),
        name="diff_attention_online",
    )(sl, qkvz3, qkvz3, qkvz3, absd, kx, lam, og)


def _prep_a(g, w_in, v_g, w_s, b_s, w_out):
    return (g[None], w_in.astype(BF16), v_g[None], w_s.astype(BF16), b_s[:, :, None], w_out.astype(BF16))


def _prep_b(g, w_in, w_gate, gate_bias, o_g, w_out):
    c_q, c_k, c_v, c_g, c_a = 0, GLA_KEY, 2 * GLA_KEY, 2 * GLA_KEY + D_INNER, 2 * GLA_KEY + 2 * D_INNER
    w_main = jnp.concatenate([w_in[:, c_v:c_g], w_in[:, c_g:c_a], w_in[:, c_q:c_k], w_in[:, c_k:c_v]],
                             axis=1).astype(BF16)
    w_a = jnp.pad(w_in[:, c_a:], ((0, 0), (0, LANES - 2 * GLA_RANK))).astype(BF16)
    wg = jnp.zeros((LANES, 2 * GLA_KEY), F32)
    wg = wg.at[:GLA_RANK, :GLA_KEY].set(w_gate[0]).at[GLA_RANK:2 * GLA_RANK, GLA_KEY:].set(w_gate[1])
    rows = np.arange(MXU_DEPTH)
    same_chunk = (rows[:, None] // GLA_CHUNK) == (rows[None, :] // GLA_CHUNK)
    tril = jnp.asarray(same_chunk & (rows[None, :] <= rows[:, None]), BF16)
    triu = jnp.asarray(same_chunk & (rows[None, :] >= rows[:, None]), BF16)
    return (g[None], w_main, w_a, wg.astype(BF16), gate_bias.reshape(1, -1), tril, triu, o_g[None],
            w_out.astype(BF16))


def _layer_b(x, g, w_main, w_a, wg, gate_bias, tril, triu, o_g, w_out):
    nb = x.shape[0] // SEQ
    main, cf, cb = _inproj_b(x, g, w_main, w_a, wg, gate_bias, tril, triu)
    o_f, o_b = _gla(main.reshape(nb, SEQ, -1), cf.reshape(nb, SEQ, -1), cb.reshape(nb, SEQ, -1))
    return _outproj_b(o_f.reshape(-1, D_INNER), o_b.reshape(-1, D_INNER), main, o_g, w_out, x)


def _prep_c(g, w_in, q_g, k_g, lam, o_g, w_out, lambda_init):
    logit_bound = jnp.max(jnp.abs(q_g)) * jnp.max(jnp.abs(k_g)) * (DIFF_DQK ** 0.5 * LOG2E * 1.02)
    return (g[None], w_in.astype(BF16), q_g[None], k_g[None], lam, o_g[None], w_out.astype(BF16),
            logit_bound <= MAX_UNSHIFTED_LOG2, lambda_init)


def _layer_c(x, g, w_in, q_g, k_g, lam, o_g, w_out, logits_bounded, lambda_init):
    nb = x.shape[0] // SEQ
    qkvz = _inproj_c(x, g, w_in, q_g, k_g)
    sl_np = (2.0 ** (-8.0 * np.arange(1, DIFF_HEADS + 1) / DIFF_HEADS) * LOG2E).astype(np.float32)
    qkvz3 = qkvz.reshape(nb, SEQ, -1)
    o = lax.cond(logits_bounded,
                 lambda: _attention_bounded(qkvz3, sl_np, lam, o_g, lambda_init),
                 lambda: _attention_online(qkvz3, jnp.asarray(sl_np), lam, o_g, lambda_init))
    return _outproj_c(o.reshape(-1, D_INNER), qkvz, w_out, x)


def kernel(x_prompt, x_sample, norm_g, a_w_in, a_v_g, a_w_s, a_b_s, a_w_out, b_w_in, b_w_gate, b_gate_bias,
           b_o_g, b_w_out, c_w_in, c_q_g, c_k_g, c_lam, c_o_g, c_w_out):
    layers = []
    for i in range(DEPTH):
        kind, j = i % 3, i // 3
        if kind == 0:
            layers.append((_layer_a_call, _prep_a(norm_g[i], a_w_in[j], a_v_g[j], a_w_s[j], a_b_s[j], a_w_out[j])))
        elif kind == 1:
            layers.append((_layer_b, _prep_b(norm_g[i], b_w_in[j], b_w_gate[j], b_gate_bias[j], b_o_g[j],
                                             b_w_out[j])))
        else:
            lambda_init = 0.8 - 0.6 * math.exp(-0.3 * i)
            layers.append((_layer_c, _prep_c(norm_g[i], c_w_in[j], c_q_g[j], c_k_g[j], c_lam[j], c_o_g[j],
                                             c_w_out[j], lambda_init)))

    def trunk(x_group):
        x = x_group.reshape(-1, D_MODEL)
        for layer, params in layers:
            x = layer(x, *params)
        return x.reshape(x_group.shape)

    return trunk(x_prompt), trunk(x_sample)
```

```python
import functools
import math

import jax
import jax.numpy as jnp
import numpy as np
from jax import lax
from jax.experimental import pallas as pl
from jax.experimental.pallas import tpu as pltpu

F32 = jnp.float32
BF16 = jnp.bfloat16

D_MODEL = 1024
SEQ = 16384
DEPTH = 4
D_INNER = 2048
EPS = 1e-6

SG_CHUNK = 128
SG_GROUPS = 8
SG_GDIM = D_INNER // SG_GROUPS

GLA_HEADS = 4
GLA_KEY = 512
GLA_DK = 128
GLA_DV = 512
GLA_RANK = 16
GLA_TAU = 16.0
GLA_CHUNK = 64

DIFF_HEADS = 8
DIFF_DQK = 128
DIFF_DV = 256

LANES = 128
MXU_DEPTH = 256
VMEM_LIMIT = 56 * 1024 * 1024

TM = 512
GLA_ROWS = 256
TQ = 1024
TK = 512
TQ_ONLINE = 512
LOG2E = math.log2(math.e)
MAX_UNSHIFTED_LOG2 = 80.0


def _cparams(sem):
    return pltpu.CompilerParams(dimension_semantics=sem, vmem_limit_bytes=VMEM_LIMIT)


def _resident(shape):
    nd = len(shape)
    return pl.BlockSpec(shape, lambda *_: (0,) * nd, pipeline_mode=pl.Buffered(1))


def _rms(x, g):
    return x * lax.rsqrt(jnp.mean(x * x, axis=-1, keepdims=True) + EPS) * g


def _silu(z):
    return z * jax.nn.sigmoid(z)


def _dot(a, b):
    return jnp.dot(a, b, preferred_element_type=F32)


def _dot_nt(a, b):
    return lax.dot_general(a, b, (((1,), (1,)), ((), ())), preferred_element_type=F32)


def _dot_tn(a, b):
    return lax.dot_general(a, b, (((0,), (0,)), ((), ())), preferred_element_type=F32)


def _layer_a_kernel(x_ref, g_ref, w_in_ref, vg_ref, ws_ref, bs_ref, w_out_ref, o_ref, uvz_ref, y_ref):
    x = x_ref[...]
    h = _rms(x, g_ref[...]).astype(BF16)
    for blk in range(3):
        lo = blk * D_INNER
        y = _dot(h, w_in_ref[:, lo:lo + D_INNER])
        if blk == 1:
            y = _rms(y, vg_ref[...])
        uvz_ref[:, lo:lo + D_INNER] = y.astype(BF16)
    for c in range(TM // SG_CHUNK):
        rows = slice(c * SG_CHUNK, (c + 1) * SG_CHUNK)
        for grp in range(SG_GROUPS):
            u_cols = slice(grp * SG_GDIM, (grp + 1) * SG_GDIM)
            v_cols = slice(D_INNER + u_cols.start, D_INNER + u_cols.stop)
            z_cols = slice(2 * D_INNER + u_cols.start, 2 * D_INNER + u_cols.stop)
            sv = _dot(ws_ref[grp], uvz_ref[rows, v_cols]) + bs_ref[grp]
            y = uvz_ref[rows, u_cols].astype(F32) * sv * _silu(uvz_ref[rows, z_cols].astype(F32))
            y_ref[rows, u_cols] = y.astype(BF16)
    o_ref[...] = x + _dot(y_ref[...], w_out_ref[...])


def _layer_a_call(x, g, w_in, vg, ws, bs, w_out):
    t = x.shape[0]
    n = 3 * D_INNER
    return pl.pallas_call(
        _layer_a_kernel,
        grid=(t // TM,),
        in_specs=[pl.BlockSpec((TM, D_MODEL), lambda i: (i, 0)),
                  _resident((1, D_MODEL)), _resident((D_MODEL, n)), _resident((1, D_INNER)),
                  _resident((SG_GROUPS, SG_CHUNK, SG_CHUNK)), _resident((SG_GROUPS, SG_CHUNK, 1)),
                  _resident((D_INNER, D_MODEL))],
        out_specs=pl.BlockSpec((TM, D_MODEL), lambda i: (i, 0)),
        out_shape=jax.ShapeDtypeStruct((t, D_MODEL), F32),
        scratch_shapes=[pltpu.VMEM((TM, n), BF16), pltpu.VMEM((TM, D_INNER), BF16)],
        compiler_params=_cparams(("parallel",)),
        name="layer_a",
    )(x, g, w_in, vg, ws, bs, w_out)


def _log_sigmoid(x):
    return -(jnp.maximum(-x, 0.0) + jnp.log1p(jnp.exp(-jnp.abs(x))))


def _split3(x):
    hi = x.astype(BF16)
    r1 = x - hi.astype(F32)
    mid = r1.astype(BF16)
    lo = (r1 - mid.astype(F32)).astype(BF16)
    return hi, mid, lo


def _inproj_b_kernel(x_ref, g_ref, w_ref, wa_ref, wg_ref, gb_ref, tril_ref, triu_ref,
                     o_ref, cf_ref, cb_ref):
    h = _rms(x_ref[...], g_ref[...]).astype(BF16)
    code = _dot(h, wa_ref[...]).astype(BF16)
    la = _log_sigmoid(_dot(code, wg_ref[...]) + gb_ref[...]) / GLA_TAU
    pieces_f = _split3(la[:, :GLA_KEY])[:2]
    pieces_b = _split3(la[:, GLA_KEY:])[:2]
    for blk in range(2):
        lo = blk * D_INNER
        o_ref[:, lo:lo + D_INNER] = _dot(h, w_ref[:, lo:lo + D_INNER]).astype(BF16)
    lo = 2 * D_INNER
    qk = _dot(h, w_ref[:, lo:lo + 2 * GLA_KEY])
    o_ref[:, lo:lo + GLA_KEY] = (qk[:, :GLA_KEY] * GLA_DK ** -0.5).astype(BF16)
    o_ref[:, lo + GLA_KEY:lo + 2 * GLA_KEY] = qk[:, GLA_KEY:].astype(BF16)
    for blk in range(TM // MXU_DEPTH):
        rows = slice(blk * MXU_DEPTH, (blk + 1) * MXU_DEPTH)
        cf_ref[rows, :] = sum(_dot(tril_ref[...], piece[rows, :]) for piece in pieces_f)
        cb_ref[rows, :] = sum(_dot(triu_ref[...], piece[rows, :]) for piece in pieces_b)


def _inproj_b(x, g, w_main, w_a, w_gate, gate_bias, tril, triu):
    t = x.shape[0]
    n = w_main.shape[1]
    return pl.pallas_call(
        _inproj_b_kernel,
        grid=(t // TM,),
        in_specs=[pl.BlockSpec((TM, D_MODEL), lambda i: (i, 0)),
                  _resident((1, D_MODEL)), _resident((D_MODEL, n)), _resident((D_MODEL, LANES)),
                  _resident((LANES, 2 * GLA_KEY)), _resident((1, 2 * GLA_KEY)),
                  _resident((MXU_DEPTH, MXU_DEPTH)), _resident((MXU_DEPTH, MXU_DEPTH))],
        out_specs=[pl.BlockSpec((TM, n), lambda i: (i, 0)),
                   pl.BlockSpec((TM, GLA_KEY), lambda i: (i, 0)),
                   pl.BlockSpec((TM, GLA_KEY), lambda i: (i, 0))],
        out_shape=[jax.ShapeDtypeStruct((t, n), BF16),
                   jax.ShapeDtypeStruct((t, GLA_KEY), F32),
                   jax.ShapeDtypeStruct((t, GLA_KEY), F32)],
        compiler_params=_cparams(("parallel",)),
        name="inproj_b",
    )(x, g, w_main, w_a, w_gate, gate_bias, tril, triu)


def _inproj_c_kernel(x_ref, g_ref, w_ref, qg_ref, kg_ref, o_ref):
    h = _rms(x_ref[...], g_ref[...]).astype(BF16)
    half = D_INNER // 2
    for blk in range(8):
        lo = blk * half
        y = _dot(h, w_ref[:, lo:lo + half])
        if blk < 4:
            gain = qg_ref[...] if blk < 2 else kg_ref[...]
            scale = DIFF_DQK ** -0.5 * LOG2E if blk < 2 else 1.0
            for grp in range(half // DIFF_DQK):
                sl = slice(grp * DIFF_DQK, (grp + 1) * DIFF_DQK)
                o_ref[:, lo + sl.start:lo + sl.stop] = (_rms(y[:, sl], gain) * scale).astype(BF16)
        else:
            o_ref[:, lo:lo + half] = y.astype(BF16)


def _inproj_c(x, g, w, qg, kg):
    t = x.shape[0]
    n = 4 * D_INNER
    return pl.pallas_call(
        _inproj_c_kernel,
        grid=(t // TM,),
        in_specs=[pl.BlockSpec((TM, D_MODEL), lambda i: (i, 0)),
                  _resident((1, D_MODEL)), _resident((D_MODEL, n)),
                  _resident((1, DIFF_DQK)), _resident((1, DIFF_DQK))],
        out_specs=pl.BlockSpec((TM, n), lambda i: (i, 0)),
        out_shape=jax.ShapeDtypeStruct((t, n), BF16),
        compiler_params=_cparams(("parallel",)),
        name="inproj_c",
    )(x, g, w, qg, kg)


def _outproj_b_kernel(of_ref, ob_ref, g_ref, og_ref, w_ref, x_ref, o_ref):
    out = x_ref[...]
    for hd in range(GLA_HEADS):
        cols = slice(hd * GLA_DV, (hd + 1) * GLA_DV)
        o = of_ref[:, cols].astype(F32) + ob_ref[:, cols].astype(F32)
        y = _rms(o, og_ref[...]) * _silu(g_ref[:, cols].astype(F32))
        out = out + _dot(y.astype(BF16), w_ref[cols, :])
    o_ref[...] = out


def _outproj_b(o_f, o_b, main, og, w, x):
    t = x.shape[0]
    return pl.pallas_call(
        _outproj_b_kernel,
        grid=(t // TM,),
        in_specs=[pl.BlockSpec((TM, D_INNER), lambda i: (i, 0)),
                  pl.BlockSpec((TM, D_INNER), lambda i: (i, 0)),
                  pl.BlockSpec((TM, D_INNER), lambda i: (i, 1)),
                  _resident((1, GLA_DV)), _resident((D_INNER, D_MODEL)),
                  pl.BlockSpec((TM, D_MODEL), lambda i: (i, 0))],
        out_specs=pl.BlockSpec((TM, D_MODEL), lambda i: (i, 0)),
        out_shape=jax.ShapeDtypeStruct((t, D_MODEL), F32),
        compiler_params=_cparams(("parallel",)),
        name="outproj_b",
    )(o_f, o_b, main, og, w, x)


def _outproj_c_kernel(a_ref, z_ref, w_ref, x_ref, o_ref):
    out = x_ref[...]
    quarter = D_INNER // 4
    for part in range(4):
        cols = slice(part * quarter, (part + 1) * quarter)
        y = (a_ref[:, cols].astype(F32) * _silu(z_ref[:, cols].astype(F32))).astype(BF16)
        out = out + _dot(y, w_ref[cols, :])
    o_ref[...] = out


def _outproj_c(o, qkvz, w, x):
    t = x.shape[0]
    return pl.pallas_call(
        _outproj_c_kernel,
        grid=(t // TM,),
        in_specs=[pl.BlockSpec((TM, D_INNER), lambda i: (i, 0)),
                  pl.BlockSpec((TM, D_INNER), lambda i: (i, 3)),
                  _resident((D_INNER, D_MODEL)),
                  pl.BlockSpec((TM, D_MODEL), lambda i: (i, 0))],
        out_specs=pl.BlockSpec((TM, D_MODEL), lambda i: (i, 0)),
        out_shape=jax.ShapeDtypeStruct((t, D_MODEL), F32),
        compiler_params=_cparams(("parallel",)),
        name="outproj_c",
    )(o, qkvz, w, x)


def _gla_pair(q_ref, k_ref, c_ref, ct_ref, v_ref, o_ref, s_ref, hd, pair, reverse):
    i_ref, i_last = (GLA_CHUNK // 2 - 1, 0) if reverse else (GLA_CHUNK // 2, GLA_CHUNK - 1)
    kcols = slice(hd * GLA_DK, (hd + 1) * GLA_DK)
    vcols = slice(hd * GLA_DV, (hd + 1) * GLA_DV)
    r0 = pair * 2 * GLA_CHUNK
    first, second = slice(r0, r0 + GLA_CHUNK), slice(r0 + GLA_CHUNK, r0 + 2 * GLA_CHUNK)
    rows_a, rows_b = (second, first) if reverse else (first, second)

    def in_memory_order(of_a, of_b):
        return jnp.concatenate([of_b, of_a] if reverse else [of_a, of_b], axis=0)

    def prepare(rows):
        q = q_ref[rows, kcols].astype(F32)
        k = k_ref[rows, kcols].astype(F32)
        cum = c_ref[rows, kcols]
        ref = cum[i_ref:i_ref + 1, :]
        last = cum[i_last:i_last + 1, :]
        return (q * jnp.exp(cum - ref), k * jnp.exp(ref - cum), q * jnp.exp(cum), k * jnp.exp(last - cum),
                jnp.exp(last))

    qg_a, kg_a, qi_a, ki_a, d_a = prepare(rows_a)
    qg_b, kg_b, qi_b, ki_b, d_b = prepare(rows_b)
    v_pair = v_ref[r0:r0 + 2 * GLA_CHUNK, vcols]
    zeros = jnp.zeros((GLA_CHUNK, GLA_DK), BF16)

    yield
    keys_for_a = in_memory_order(kg_a.astype(BF16), kg_b.astype(BF16))
    w_a = _dot_nt(qg_a.astype(BF16), keys_for_a)
    keys_for_b = in_memory_order(jnp.concatenate([zeros, ki_a.astype(BF16)], axis=1),
                                 jnp.concatenate([kg_b.astype(BF16), zeros], axis=1))
    w_b = _dot_nt(jnp.concatenate([qg_b, qi_b], axis=1).astype(BF16), keys_for_b)
    row = lax.broadcasted_iota(jnp.int32, (GLA_CHUNK, 2 * GLA_CHUNK), 0)
    col = lax.broadcasted_iota(jnp.int32, (GLA_CHUNK, 2 * GLA_CHUNK), 1)
    if reverse:
        keep_a = (col >= GLA_CHUNK) & (row < col - GLA_CHUNK)
        keep_b = (col >= GLA_CHUNK) | (row < col)
    else:
        keep_a = (col < GLA_CHUNK) & (row >= col)
        keep_b = (col < GLA_CHUNK) | (row >= col - GLA_CHUNK)
    w_a = jnp.where(keep_a, w_a, 0.0).astype(BF16)
    w_b = jnp.where(keep_b, w_b, 0.0).astype(BF16)

    lhs = jnp.concatenate([in_memory_order(qi_a.astype(BF16), (qi_b * d_a).astype(BF16)),
                           in_memory_order(w_a, w_b)], axis=1)
    last_a = rows_a.start + i_last
    last_b = rows_b.start + i_last
    decay = jnp.exp(ct_ref[kcols, last_a:last_a + 1] + ct_ref[kcols, last_b:last_b + 1])
    keys_to_state = in_memory_order((ki_a * d_b).astype(BF16), ki_b.astype(BF16))

    yield
    state = s_ref[hd]
    rhs = jnp.concatenate([state.astype(BF16), v_pair], axis=0)
    o_ref[r0:r0 + 2 * GLA_CHUNK, vcols] = _dot(lhs, rhs).astype(o_ref.dtype)
    s_ref[hd] = state * decay + _dot_tn(keys_to_state, v_pair)
    yield


def _gla_kernel(qf_ref, kf_ref, cf_ref, vf_ref, qb_ref, kb_ref, cb_ref, vb_ref,
                of_ref, ob_ref, sf_ref, sb_ref, ctf_ref, ctb_ref):
    @pl.when(pl.program_id(1) == 0)
    def _():
        sf_ref[...] = jnp.zeros_like(sf_ref)
        sb_ref[...] = jnp.zeros_like(sb_ref)

    ctf_ref[...] = cf_ref[...].T
    ctb_ref[...] = cb_ref[...].T
    n_pairs = GLA_ROWS // (2 * GLA_CHUNK)
    pairs = []
    for p in range(n_pairs):
        for hd in range(GLA_HEADS):
            pairs.append(_gla_pair(qf_ref, kf_ref, cf_ref, ctf_ref, vf_ref, of_ref, sf_ref, hd, p, reverse=False))
            pairs.append(_gla_pair(qb_ref, kb_ref, cb_ref, ctb_ref, vb_ref, ob_ref, sb_ref, hd, n_pairs - 1 - p,
                                   reverse=True))
    for _stage in range(3):
        for pair in pairs:
            next(pair)


def _gla(main3, cf3, cb3):
    nb, s, _ = main3.shape
    steps = s // GLA_ROWS
    q_col = 2 * D_INNER // GLA_KEY
    k_col = q_col + 1

    def fwd(col):
        return lambda b, i: (b, i, col)

    def bwd(col):
        return lambda b, i: (b, steps - 1 - i, col)

    def specs(idx):
        return [pl.BlockSpec((None, GLA_ROWS, GLA_KEY), idx(q_col)),
                pl.BlockSpec((None, GLA_ROWS, GLA_KEY), idx(k_col)),
                pl.BlockSpec((None, GLA_ROWS, GLA_KEY), idx(0)),
                pl.BlockSpec((None, GLA_ROWS, D_INNER), idx(0))]

    out_sd = jax.ShapeDtypeStruct((nb, s, D_INNER), BF16)
    state = pltpu.VMEM((GLA_HEADS, GLA_DK, GLA_DV), F32)
    cum_t = pltpu.VMEM((GLA_KEY, GLA_ROWS), F32)
    return pl.pallas_call(
        _gla_kernel,
        grid=(nb, steps),
        in_specs=specs(fwd) + specs(bwd),
        out_specs=[pl.BlockSpec((None, GLA_ROWS, D_INNER), fwd(0)),
                   pl.BlockSpec((None, GLA_ROWS, D_INNER), bwd(0))],
        out_shape=[out_sd, out_sd],
        scratch_shapes=[state, state, cum_t, cum_t],
        compiler_params=_cparams(("parallel", "arbitrary")),
        name="gla_scan",
    )(main3, main3, cf3, main3, main3, main3, cb3, main3)


def _attn_finish(acc_ref, l0, l1, lam_ref, og_ref, o_ref, lambda_init):
    lam = lam_ref[...]
    lam_full = (jnp.exp(jnp.sum(lam[0:1] * lam[1:2], axis=1, keepdims=True))
                - jnp.exp(jnp.sum(lam[2:3] * lam[3:4], axis=1, keepdims=True)) + lambda_init)
    o = acc_ref[0] * (1.0 / l0) - lam_full * (acc_ref[1] * (1.0 / l1))
    o_ref[...] = (_rms(o, og_ref[...]) * (1.0 - lambda_init)).astype(o_ref.dtype)


def _lane_select(lane, pieces):
    out = 0.0
    for c, piece in reversed(list(enumerate(pieces))):
        out = jnp.where(lane == c, piece, out)
    return out


def _attn_bounded_kernel(sl_ref, q_ref, k_ref, v_ref, qx_ref, kx_ref, dist_ref, lam_ref, og_ref, o_ref,
                         qa_ref, p_ref, l_ref, acc_ref, *, lambda_init):
    hd = pl.program_id(1)
    qi = pl.program_id(2)
    ratio = TQ // TK
    n_off = k_ref.shape[0] // TK - ratio
    assert ratio == 2 and n_off >= 2 and n_off % 2 == 0
    sl = sl_ref[hd]

    for m in range(2):
        qa_ref[m, :, :DIFF_DQK] = q_ref[:, m * DIFF_DQK:(m + 1) * DIFF_DQK]
        qa_ref[m, :, DIFF_DQK:] = qx_ref[...]
    l_ref[...] = jnp.zeros_like(l_ref)
    acc_ref[...] = jnp.zeros_like(acc_ref)
    lane8 = lax.broadcasted_iota(jnp.int32, (8, LANES), 1)

    def tile_of(t):
        is_left = t < ratio * qi
        return is_left, jnp.where(is_left, t, t + ratio)

    def probabilities(slot, logits_of_map):
        for m in range(2):
            p = jnp.exp2(logits_of_map(m))
            part = p[:, :LANES]
            for c in range(1, TK // LANES):
                part = part + p[:, c * LANES:(c + 1) * LANES]
            l_ref[m] += part
            p_ref[slot, m] = p.astype(BF16)

    def off_diag(t, slot):
        is_left, j = tile_of(t)
        gap = jnp.where(is_left, qi * TQ - (j + 1) * TK, j * TK - (qi + 1) * TQ).astype(F32)
        gap_pieces = [p.astype(F32) for p in _split3(jnp.full((8, LANES), sl * gap, F32))]
        gap_row = _lane_select(lane8, [0.0] * 9 + gap_pieces)[0:1, :].astype(BF16)
        k_ext = kx_ref[jnp.where(is_left, 0, 1)] + gap_row
        k = k_ref[pl.ds(pl.multiple_of(j * TK, TK), TK), :]

        def logits(m):
            ka = jnp.concatenate([k[:, m * DIFF_DQK:(m + 1) * DIFF_DQK], k_ext], axis=1)
            return _dot_nt(qa_ref[m], ka)

        probabilities(slot, logits)

    def diag(d, slot):
        k = k_ref[pl.ds(pl.multiple_of((ratio * qi + d) * TK, TK), TK), :]
        bias = sl * dist_ref[d]

        def logits(m):
            cols = slice(m * DIFF_DQK, (m + 1) * DIFF_DQK)
            return _dot_nt(q_ref[:, cols], k[:, cols]) - bias

        probabilities(slot, logits)

    def weighted_values(j, slot):
        v = v_ref[pl.ds(pl.multiple_of(j * TK, TK), TK), :]
        for m in range(2):
            acc_ref[m] += _dot(p_ref[slot, m], v)

    off_diag(0, 0)

    def pair(u, carry):
        t = 2 * u
        weighted_values(tile_of(t)[1], 0)
        off_diag(t + 1, 1)
        weighted_values(tile_of(t + 1)[1], 1)
        off_diag(t + 2, 0)
        return carry

    lax.fori_loop(0, (n_off - 2) // 2, pair, 0, unroll=7)
    weighted_values(tile_of(n_off - 2)[1], 0)
    off_diag(n_off - 1, 1)
    weighted_values(tile_of(n_off - 1)[1], 1)
    diag(0, 0)
    weighted_values(ratio * qi, 0)
    diag(1, 1)
    weighted_values(ratio * qi + 1, 1)

    _attn_finish(acc_ref, jnp.sum(l_ref[0], axis=1, keepdims=True), jnp.sum(l_ref[1], axis=1, keepdims=True),
                 lam_ref, og_ref, o_ref, lambda_init)


def _split3_np(x):
    x = np.asarray(x, np.float32)
    hi = x.astype(BF16).astype(np.float32)
    mid = (x - hi).astype(BF16).astype(np.float32)
    lo = (x - hi - mid).astype(BF16).astype(np.float32)
    return [hi, mid, lo]


def _alibi_tables(sl_np):
    nh = len(sl_np)
    ii = np.arange(TQ, dtype=np.float32)
    jj = np.arange(TK, dtype=np.float32)
    qx = np.zeros((nh, TQ, LANES), np.float32)
    kx = np.zeros((nh, 2, TK, LANES), np.float32)
    for h, s in enumerate(np.asarray(sl_np, np.float32)):
        qx[h, :, 0:3] = -1.0
        qx[h, :, 9:12] = -1.0
        for c, piece in enumerate(_split3_np(s * ii)):
            qx[h, :, 3 + c] = piece
        for c, piece in enumerate(_split3_np(s * (TQ - ii))):
            qx[h, :, 6 + c] = piece
        for c, piece in enumerate(_split3_np(s * (TK - jj))):
            kx[h, 0, :, c] = piece
        for c, piece in enumerate(_split3_np(s * jj)):
            kx[h, 1, :, c] = piece
        kx[h, 0, :, 3:6] = -1.0
        kx[h, 1, :, 6:9] = -1.0
    return jnp.asarray(qx, BF16), jnp.asarray(kx, BF16)


def _attention_bounded(qkvz3, sl_np, lam, og, lambda_init):
    nb, s, _ = qkvz3.shape
    k_col0 = D_INNER // DIFF_DV
    v_col0 = 2 * k_col0
    qx, kx = _alibi_tables(sl_np)
    i_loc = np.arange(TQ)[None, :, None]
    j_abs = np.arange(TK)[None, None, :] + TK * np.arange(TQ // TK)[:, None, None]
    dist = np.abs(i_loc - j_abs).astype(np.float32)
    return pl.pallas_call(
        functools.partial(_attn_bounded_kernel, lambda_init=lambda_init),
        grid=(nb, DIFF_HEADS, s // TQ),
        in_specs=[pl.BlockSpec(memory_space=pltpu.SMEM),
                  pl.BlockSpec((None, TQ, DIFF_DV), lambda b, h, i: (b, i, h)),
                  pl.BlockSpec((None, s, DIFF_DV), lambda b, h, i: (b, 0, k_col0 + h)),
                  pl.BlockSpec((None, s, DIFF_DV), lambda b, h, i: (b, 0, v_col0 + h)),
                  pl.BlockSpec((None, TQ, LANES), lambda b, h, i: (h, 0, 0)),
                  pl.BlockSpec((None, 2, TK, LANES), lambda b, h, i: (h, 0, 0, 0)),
                  _resident((TQ // TK, TQ, TK)),
                  _resident((4, DIFF_DQK)), _resident((1, DIFF_DV))],
        out_specs=pl.BlockSpec((None, TQ, DIFF_DV), lambda b, h, i: (b, i, h)),
        out_shape=jax.ShapeDtypeStruct((nb, s, D_INNER), BF16),
        scratch_shapes=[pltpu.VMEM((2, TQ, 2 * DIFF_DQK), BF16),
                        pltpu.VMEM((2, 2, TQ, TK), BF16),
                        pltpu.VMEM((2, TQ, LANES), F32),
                        pltpu.VMEM((2, TQ, DIFF_DV), F32)],
        compiler_params=_cparams(("parallel", "parallel", "arbitrary")),
        name="diff_attention_bounded",
    )(jnp.asarray(sl_np, F32), qkvz3, qkvz3, qkvz3, qx, kx, jnp.asarray(dist), lam, og)


def _attn_online_kernel(sl_ref, q_ref, k_ref, v_ref, absd_ref, kx_ref, lam_ref, og_ref, o_ref,
                        ql_ref, qr_ref, m_ref, l_ref, acc_ref, *, lambda_init):
    TQ = TQ_ONLINE
    hd = pl.program_id(1)
    qi = pl.program_id(2)
    n_kv = k_ref.shape[0] // TQ
    slope = sl_ref[hd]

    lane = lax.broadcasted_iota(jnp.int32, (TQ, LANES), 1)
    pieces = [p.astype(F32) for p in _split3(jnp.full((TQ, LANES), slope, F32))]
    ext = _lane_select(lane, pieces + pieces).astype(BF16)
    for m in range(2):
        qm = q_ref[:, m * DIFF_DQK:(m + 1) * DIFF_DQK]
        ql_ref[m, :, :DIFF_DQK] = qm
        ql_ref[m, :, DIFF_DQK:] = ext
        qr_ref[m, :, :DIFF_DQK] = qm
        qr_ref[m, :, DIFF_DQK:] = -ext
    m_ref[...] = jnp.full(m_ref.shape, -1e30, F32)
    l_ref[...] = jnp.zeros_like(l_ref)
    acc_ref[...] = jnp.zeros_like(acc_ref)
    ii = lax.broadcasted_iota(jnp.int32, (TQ, 1), 0).astype(F32)

    def update(m, s, r, v):
        m_old = m_ref[m]
        m_new = jnp.maximum(m_old, jnp.max(s, axis=1, keepdims=True) + r)
        alpha = jnp.exp2(m_old - m_new)
        p = jnp.exp2(s - (m_new - r))
        l_ref[m] = alpha * l_ref[m] + jnp.sum(p, axis=1, keepdims=True)
        acc_ref[m] = alpha * acc_ref[m] + _dot(p.astype(BF16), v)
        m_ref[m] = m_new

    def off_diag(j, qa_ref, r):
        start = pl.multiple_of(j * TQ, TQ)
        k = k_ref[pl.ds(start, TQ), :]
        v = v_ref[pl.ds(start, TQ), :]
        for m in range(2):
            ka = jnp.concatenate([k[:, m * DIFF_DQK:(m + 1) * DIFF_DQK], kx_ref[...]], axis=1)
            update(m, _dot_nt(qa_ref[m], ka), r, v)

    def left(j, carry):
        off_diag(j, ql_ref, -slope * (((qi - j) * TQ).astype(F32) + ii))
        return carry

    def right(j, carry):
        off_diag(j, qr_ref, slope * (ii - ((j - qi) * TQ).astype(F32)))
        return carry

    lax.fori_loop(0, qi, left, 0)

    start = pl.multiple_of(qi * TQ, TQ)
    k = k_ref[pl.ds(start, TQ), :]
    v = v_ref[pl.ds(start, TQ), :]
    bias = -slope * absd_ref[...]
    zero = jnp.zeros((TQ, 1), F32)
    for m in range(2):
        cols = slice(m * DIFF_DQK, (m + 1) * DIFF_DQK)
        update(m, _dot_nt(q_ref[:, cols], k[:, cols]) + bias, zero, v)

    lax.fori_loop(qi + 1, n_kv, right, 0)

    _attn_finish(acc_ref, l_ref[0], l_ref[1], lam_ref, og_ref, o_ref, lambda_init)


def _attention_online(qkvz3, sl, lam, og, lambda_init):
    TQ = TQ_ONLINE
    nb, s, _ = qkvz3.shape
    k_col0 = D_INNER // DIFF_DV
    v_col0 = 2 * k_col0
    pos = np.arange(TQ)
    absd = jnp.asarray(np.abs(pos[:, None] - pos[None, :]), F32)
    kx_np = np.zeros((TQ, LANES), np.float32)
    kx_np[:, 0:3] = ((pos // 16) * 16)[:, None]
    kx_np[:, 3:6] = (pos % 16)[:, None]
    kx = jnp.asarray(kx_np, BF16)
    return pl.pallas_call(
        functools.partial(_attn_online_kernel, lambda_init=lambda_init),
        grid=(nb, DIFF_HEADS, s // TQ),
        in_specs=[pl.BlockSpec(memory_space=pltpu.SMEM),
                  pl.BlockSpec((None, TQ, DIFF_DV), lambda b, h, i: (b, i, h)),
                  pl.BlockSpec((None, s, DIFF_DV), lambda b, h, i: (b, 0, k_col0 + h)),
                  pl.BlockSpec((None, s, DIFF_DV), lambda b, h, i: (b, 0, v_col0 + h)),
                  _resident((TQ, TQ)), _resident((TQ, LANES)),
                  _resident((4, DIFF_DQK)), _resident((1, DIFF_DV))],
        out_specs=pl.BlockSpec((None, TQ, DIFF_DV), lambda b, h, i: (b, i, h)),
        out_shape=jax.ShapeDtypeStruct((nb, s, D_INNER), BF16),
        scratch_shapes=[pltpu.VMEM((2, TQ, 2 * DIFF_DQK), BF16), pltpu.VMEM((2, TQ, 2 * DIFF_DQK), BF16),
                        pltpu.VMEM((2, TQ, 1), F32), pltpu.VMEM((2, TQ, 1), F32),
                        pltpu.VMEM((2, TQ, DIFF_DV), F32)],
        compiler_params=_cparams(("parallel", "parallel", "arbitrary")),
        name="diff_attention_online",
    )(sl, qkvz3, qkvz3, qkvz3, absd, kx, lam, og)


def _prep_a(g, w_in, v_g, w_s, b_s, w_out):
    return (g[None], w_in.astype(BF16), v_g[None], w_s.astype(BF16), b_s[:, :, None], w_out.astype(BF16))


def _prep_b(g, w_in, w_gate, gate_bias, o_g, w_out):
    c_q, c_k, c_v, c_g, c_a = 0, GLA_KEY, 2 * GLA_KEY, 2 * GLA_KEY + D_INNER, 2 * GLA_KEY + 2 * D_INNER
    w_main = jnp.concatenate([w_in[:, c_v:c_g], w_in[:, c_g:c_a], w_in[:, c_q:c_k], w_in[:, c_k:c_v]],
                             axis=1).astype(BF16)
    w_a = jnp.pad(w_in[:, c_a:], ((0, 0), (0, LANES - 2 * GLA_RANK))).astype(BF16)
    wg = jnp.zeros((LANES, 2 * GLA_KEY), F32)
    wg = wg.at[:GLA_RANK, :GLA_KEY].set(w_gate[0]).at[GLA_RANK:2 * GLA_RANK, GLA_KEY:].set(w_gate[1])
    rows = np.arange(MXU_DEPTH)
    same_chunk = (rows[:, None] // GLA_CHUNK) == (rows[None, :] // GLA_CHUNK)
    tril = jnp.asarray(same_chunk & (rows[None, :] <= rows[:, None]), BF16)
    triu = jnp.asarray(same_chunk & (rows[None, :] >= rows[:, None]), BF16)
    return (g[None], w_main, w_a, wg.astype(BF16), gate_bias.reshape(1, -1), tril, triu, o_g[None],
            w_out.astype(BF16))


def _layer_b(x, g, w_main, w_a, wg, gate_bias, tril, triu, o_g, w_out):
    nb = x.shape[0] // SEQ
    main, cf, cb = _inproj_b(x, g, w_main, w_a, wg, gate_bias, tril, triu)
    o_f, o_b = _gla(main.reshape(nb, SEQ, -1), cf.reshape(nb, SEQ, -1), cb.reshape(nb, SEQ, -1))
    return _outproj_b(o_f.reshape(-1, D_INNER), o_b.reshape(-1, D_INNER), main, o_g, w_out, x)


def _prep_c(g, w_in, q_g, k_g, lam, o_g, w_out, lambda_init):
    logit_bound = jnp.max(jnp.abs(q_g)) * jnp.max(jnp.abs(k_g)) * (DIFF_DQK ** 0.5 * LOG2E * 1.02)
    return (g[None], w_in.astype(BF16), q_g[None], k_g[None], lam, o_g[None], w_out.astype(BF16),
            logit_bound <= MAX_UNSHIFTED_LOG2, lambda_init)


def _layer_c(x, g, w_in, q_g, k_g, lam, o_g, w_out, logits_bounded, lambda_init):
    nb = x.shape[0] // SEQ
    qkvz = _inproj_c(x, g, w_in, q_g, k_g)
    sl_np = (2.0 ** (-8.0 * np.arange(1, DIFF_HEADS + 1) / DIFF_HEADS) * LOG2E).astype(np.float32)
    qkvz3 = qkvz.reshape(nb, SEQ, -1)
    o = lax.cond(logits_bounded,
                 lambda: _attention_bounded(qkvz3, sl_np, lam, o_g, lambda_init),
                 lambda: _attention_online(qkvz3, jnp.asarray(sl_np), lam, o_g, lambda_init))
    return _outproj_c(o.reshape(-1, D_INNER), qkvz, w_out, x)


def kernel(x_prompt, x_sample, norm_g, a_w_in, a_v_g, a_w_s, a_b_s, a_w_out, b_w_in, b_w_gate, b_gate_bias,
           b_o_g, b_w_out, c_w_in, c_q_g, c_k_g, c_lam, c_o_g, c_w_out):
    layers = []
    for i in range(DEPTH):
        kind, j = i % 3, i // 3
        if kind == 0:
            layers.append((_layer_a_call, _prep_a(norm_g[i], a_w_in[j], a_v_g[j], a_w_s[j], a_b_s[j], a_w_out[j])))
        elif kind == 1:
            layers.append((_layer_b, _prep_b(norm_g[i], b_w_in[j], b_w_gate[j], b_gate_bias[j], b_o_g[j],
                                             b_w_out[j])))
        else:
            lambda_init = 0.8 - 0.6 * math.exp(-0.3 * i)
            layers.append((_layer_c, _prep_c(norm_g[i], c_w_in[j], c_q_g[j], c_k_g[j], c_lam[j], c_o_g[j],
                                             c_w_out[j], lambda_init)))

    def trunk(x_group):
        x = x_group.reshape(-1, D_MODEL)
        for layer, params in layers:
            x = layer(x, *params)
        return x.reshape(x_group.shape)

    return trunk(x_prompt), trunk(x_sample)
```

```python
import functools
import math

import jax
import jax.numpy as jnp
import numpy as np
from jax import lax
from jax.experimental import pallas as pl
from jax.experimental.pallas import tpu as pltpu

F32 = jnp.float32
BF16 = jnp.bfloat16

D_MODEL = 1024
SEQ = 16384
DEPTH = 4
D_INNER = 2048
EPS = 1e-6

SG_CHUNK = 128
SG_GROUPS = 8
SG_GDIM = D_INNER // SG_GROUPS

GLA_HEADS = 4
GLA_KEY = 512
GLA_DK = 128
GLA_DV = 512
GLA_RANK = 16
GLA_TAU = 16.0
GLA_CHUNK = 64

DIFF_HEADS = 8
DIFF_DQK = 128
DIFF_DV = 256

LANES = 128
MXU_DEPTH = 256
VMEM_LIMIT = 56 * 1024 * 1024

TM = 512
RING_SLOTS = 3
GLA_ROWS = 256
TQ = 1024
TK = 512
TQ_ONLINE = 512
LOG2E = math.log2(math.e)
MAX_UNSHIFTED_LOG2 = 80.0


def _cparams(sem):
    return pltpu.CompilerParams(dimension_semantics=sem, vmem_limit_bytes=VMEM_LIMIT)


def _resident(shape):
    nd = len(shape)
    return pl.BlockSpec(shape, lambda *_: (0,) * nd, pipeline_mode=pl.Buffered(1))


def _rms(x, g):
    return x * lax.rsqrt(jnp.mean(x * x, axis=-1, keepdims=True) + EPS) * g


def _silu(z):
    return z * jax.nn.sigmoid(z)


def _dot(a, b):
    return jnp.dot(a, b, preferred_element_type=F32)


def _dot_nt(a, b):
    return lax.dot_general(a, b, (((1,), (1,)), ((), ())), preferred_element_type=F32)


def _dot_tn(a, b):
    return lax.dot_general(a, b, (((0,), (0,)), ((), ())), preferred_element_type=F32)


def _layer_a_kernel(x_ref, g_ref, w_in_ref, vg_ref, ws_ref, bs_ref, w_out_ref, o_ref, uvz_ref, y_ref):
    x = x_ref[...]
    h = _rms(x, g_ref[...]).astype(BF16)
    for blk in range(3):
        lo = blk * D_INNER
        y = _dot(h, w_in_ref[:, lo:lo + D_INNER])
        if blk == 1:
            y = _rms(y, vg_ref[...])
        uvz_ref[:, lo:lo + D_INNER] = y.astype(BF16)
    for c in range(TM // SG_CHUNK):
        rows = slice(c * SG_CHUNK, (c + 1) * SG_CHUNK)
        for grp in range(SG_GROUPS):
            u_cols = slice(grp * SG_GDIM, (grp + 1) * SG_GDIM)
            v_cols = slice(D_INNER + u_cols.start, D_INNER + u_cols.stop)
            z_cols = slice(2 * D_INNER + u_cols.start, 2 * D_INNER + u_cols.stop)
            sv = _dot(ws_ref[grp], uvz_ref[rows, v_cols]) + bs_ref[grp]
            y = uvz_ref[rows, u_cols].astype(F32) * sv * _silu(uvz_ref[rows, z_cols].astype(F32))
            y_ref[rows, u_cols] = y.astype(BF16)
    o_ref[...] = x + _dot(y_ref[...], w_out_ref[...])


def _layer_a_call(x, g, w_in, vg, ws, bs, w_out):
    t = x.shape[0]
    n = 3 * D_INNER
    return pl.pallas_call(
        _layer_a_kernel,
        grid=(t // TM,),
        in_specs=[pl.BlockSpec((TM, D_MODEL), lambda i: (i, 0)),
                  _resident((1, D_MODEL)), _resident((D_MODEL, n)), _resident((1, D_INNER)),
                  _resident((SG_GROUPS, SG_CHUNK, SG_CHUNK)), _resident((SG_GROUPS, SG_CHUNK, 1)),
                  _resident((D_INNER, D_MODEL))],
        out_specs=pl.BlockSpec((TM, D_MODEL), lambda i: (i, 0)),
        out_shape=jax.ShapeDtypeStruct((t, D_MODEL), F32),
        scratch_shapes=[pltpu.VMEM((TM, n), BF16), pltpu.VMEM((TM, D_INNER), BF16)],
        compiler_params=_cparams(("parallel",)),
        name="layer_a",
    )(x, g, w_in, vg, ws, bs, w_out)


def _log_sigmoid(x):
    return -(jnp.maximum(-x, 0.0) + jnp.log1p(jnp.exp(-jnp.abs(x))))


def _split3(x):
    hi = x.astype(BF16)
    r1 = x - hi.astype(F32)
    mid = r1.astype(BF16)
    lo = (r1 - mid.astype(F32)).astype(BF16)
    return hi, mid, lo


def _inproj_b_kernel(x_ref, g_ref, w_ref, wa_ref, wg_ref, gb_ref, tril_ref, triu_ref,
                     o_ref, cf_ref, cb_ref):
    h = _rms(x_ref[...], g_ref[...]).astype(BF16)
    code = _dot(h, wa_ref[...]).astype(BF16)
    la = _log_sigmoid(_dot(code, wg_ref[...]) + gb_ref[...]) / GLA_TAU
    pieces_f = _split3(la[:, :GLA_KEY])[:2]
    pieces_b = _split3(la[:, GLA_KEY:])[:2]
    for blk in range(2):
        lo = blk * D_INNER
        o_ref[:, lo:lo + D_INNER] = _dot(h, w_ref[:, lo:lo + D_INNER]).astype(BF16)
    lo = 2 * D_INNER
    qk = _dot(h, w_ref[:, lo:lo + 2 * GLA_KEY])
    o_ref[:, lo:lo + GLA_KEY] = (qk[:, :GLA_KEY] * GLA_DK ** -0.5).astype(BF16)
    o_ref[:, lo + GLA_KEY:lo + 2 * GLA_KEY] = qk[:, GLA_KEY:].astype(BF16)
    for blk in range(TM // MXU_DEPTH):
        rows = slice(blk * MXU_DEPTH, (blk + 1) * MXU_DEPTH)
        cf_ref[rows, :] = sum(_dot(tril_ref[...], piece[rows, :]) for piece in pieces_f)
        cb_ref[rows, :] = sum(_dot(triu_ref[...], piece[rows, :]) for piece in pieces_b)


def _inproj_b(x, g, w_main, w_a, w_gate, gate_bias, tril, triu):
    t = x.shape[0]
    n = w_main.shape[1]
    return pl.pallas_call(
        _inproj_b_kernel,
        grid=(t // TM,),
        in_specs=[pl.BlockSpec((TM, D_MODEL), lambda i: (i, 0)),
                  _resident((1, D_MODEL)), _resident((D_MODEL, n)), _resident((D_MODEL, LANES)),
                  _resident((LANES, 2 * GLA_KEY)), _resident((1, 2 * GLA_KEY)),
                  _resident((MXU_DEPTH, MXU_DEPTH)), _resident((MXU_DEPTH, MXU_DEPTH))],
        out_specs=[pl.BlockSpec((TM, n), lambda i: (i, 0)),
                   pl.BlockSpec((TM, GLA_KEY), lambda i: (i, 0)),
                   pl.BlockSpec((TM, GLA_KEY), lambda i: (i, 0))],
        out_shape=[jax.ShapeDtypeStruct((t, n), BF16),
                   jax.ShapeDtypeStruct((t, GLA_KEY), F32),
                   jax.ShapeDtypeStruct((t, GLA_KEY), F32)],
        compiler_params=_cparams(("parallel",)),
        name="inproj_b",
    )(x, g, w_main, w_a, w_gate, gate_bias, tril, triu)


def _inproj_c_kernel(x_ref, g_ref, w_ref, qg_ref, kg_ref, o_ref):
    h = _rms(x_ref[...], g_ref[...]).astype(BF16)
    half = D_INNER // 2
    for blk in range(8):
        lo = blk * half
        y = _dot(h, w_ref[:, lo:lo + half])
        if blk < 4:
            gain = qg_ref[...] if blk < 2 else kg_ref[...]
            scale = DIFF_DQK ** -0.5 * LOG2E if blk < 2 else 1.0
            for grp in range(half // DIFF_DQK):
                sl = slice(grp * DIFF_DQK, (grp + 1) * DIFF_DQK)
                o_ref[:, lo + sl.start:lo + sl.stop] = (_rms(y[:, sl], gain) * scale).astype(BF16)
        else:
            o_ref[:, lo:lo + half] = y.astype(BF16)


def _inproj_c(x, g, w, qg, kg):
    t = x.shape[0]
    n = 4 * D_INNER
    return pl.pallas_call(
        _inproj_c_kernel,
        grid=(t // TM,),
        in_specs=[pl.BlockSpec((TM, D_MODEL), lambda i: (i, 0)),
                  _resident((1, D_MODEL)), _resident((D_MODEL, n)),
                  _resident((1, DIFF_DQK)), _resident((1, DIFF_DQK))],
        out_specs=pl.BlockSpec((TM, n), lambda i: (i, 0)),
        out_shape=jax.ShapeDtypeStruct((t, n), BF16),
        compiler_params=_cparams(("parallel",)),
        name="inproj_c",
    )(x, g, w, qg, kg)


def _ring_fetch(streams, bufs, sem):
    step = pl.program_id(0)
    n_steps = pl.num_programs(0)

    def copies(s):
        slot = s % RING_SLOTS
        rows = pl.ds(pl.multiple_of(s * TM, TM), TM)
        return [pltpu.make_async_copy(hbm.at[rows, pl.ds(col0, buf.shape[2])], buf.at[slot], sem.at[k, slot])
                for k, ((hbm, col0), buf) in enumerate(zip(streams, bufs))]

    @pl.when(step == 0)
    def _():
        for first in range(RING_SLOTS - 1):
            for copy in copies(first):
                copy.start()

    @pl.when(step + RING_SLOTS - 1 < n_steps)
    def _():
        for copy in copies(step + RING_SLOTS - 1):
            copy.start()

    for copy in copies(step):
        copy.wait()
    return step % RING_SLOTS


def _ring_call(kernel, streams, residents, t, name):
    assert t // TM >= RING_SLOTS
    scratch = [pltpu.VMEM((RING_SLOTS, TM, width), arr.dtype) for arr, _, width in streams]
    scratch.append(pltpu.SemaphoreType.DMA((len(streams), RING_SLOTS)))
    return pl.pallas_call(
        functools.partial(kernel, cols=tuple(col0 for _, col0, _ in streams)),
        grid=(t // TM,),
        in_specs=[pl.BlockSpec(memory_space=pl.ANY)] * len(streams) + [_resident(r.shape) for r in residents],
        out_specs=pl.BlockSpec((TM, D_MODEL), lambda i: (i, 0)),
        out_shape=jax.ShapeDtypeStruct((t, D_MODEL), F32),
        scratch_shapes=scratch,
        compiler_params=_cparams(("arbitrary",)),
        name=name,
    )(*[arr for arr, _, _ in streams], *residents)


def _outproj_b_kernel(of_hbm, ob_hbm, g_hbm, x_hbm, og_ref, w_ref, o_ref, of_buf, ob_buf, g_buf, x_buf, sem, *,
                      cols):
    slot = _ring_fetch(list(zip((of_hbm, ob_hbm, g_hbm, x_hbm), cols)), (of_buf, ob_buf, g_buf, x_buf), sem)
    out = x_buf[slot]
    for hd in range(GLA_HEADS):
        cols_h = slice(hd * GLA_DV, (hd + 1) * GLA_DV)
        o = of_buf[slot, :, cols_h].astype(F32) + ob_buf[slot, :, cols_h].astype(F32)
        y = _rms(o, og_ref[...]) * _silu(g_buf[slot, :, cols_h].astype(F32))
        out = out + _dot(y.astype(BF16), w_ref[cols_h, :])
    o_ref[...] = out


def _outproj_b(o_f, o_b, main, og, w, x):
    streams = [(o_f, 0, D_INNER), (o_b, 0, D_INNER), (main, D_INNER, D_INNER), (x, 0, D_MODEL)]
    return _ring_call(_outproj_b_kernel, streams, [og, w], x.shape[0], "outproj_b")


def _outproj_c_kernel(a_hbm, z_hbm, x_hbm, w_ref, o_ref, a_buf, z_buf, x_buf, sem, *, cols):
    slot = _ring_fetch(list(zip((a_hbm, z_hbm, x_hbm), cols)), (a_buf, z_buf, x_buf), sem)
    out = x_buf[slot]
    quarter = D_INNER // 4
    for part in range(4):
        cols_p = slice(part * quarter, (part + 1) * quarter)
        y = (a_buf[slot, :, cols_p].astype(F32) * _silu(z_buf[slot, :, cols_p].astype(F32))).astype(BF16)
        out = out + _dot(y, w_ref[cols_p, :])
    o_ref[...] = out


def _outproj_c(o, qkvz, w, x):
    streams = [(o, 0, D_INNER), (qkvz, 3 * D_INNER, D_INNER), (x, 0, D_MODEL)]
    return _ring_call(_outproj_c_kernel, streams, [w], x.shape[0], "outproj_c")


def _gla_pair(q_ref, k_ref, c_ref, ct_ref, v_ref, o_ref, s_ref, hd, pair, reverse):
    i_ref, i_last = (GLA_CHUNK // 2 - 1, 0) if reverse else (GLA_CHUNK // 2, GLA_CHUNK - 1)
    kcols = slice(hd * GLA_DK, (hd + 1) * GLA_DK)
    vcols = slice(hd * GLA_DV, (hd + 1) * GLA_DV)
    r0 = pair * 2 * GLA_CHUNK
    first, second = slice(r0, r0 + GLA_CHUNK), slice(r0 + GLA_CHUNK, r0 + 2 * GLA_CHUNK)
    rows_a, rows_b = (second, first) if reverse else (first, second)

    def in_memory_order(of_a, of_b):
        return jnp.concatenate([of_b, of_a] if reverse else [of_a, of_b], axis=0)

    def prepare(rows):
        q = q_ref[rows, kcols].astype(F32)
        k = k_ref[rows, kcols].astype(F32)
        cum = c_ref[rows, kcols]
        ref = cum[i_ref:i_ref + 1, :]
        last = cum[i_last:i_last + 1, :]
        return (q * jnp.exp(cum - ref), k * jnp.exp(ref - cum), q * jnp.exp(cum), k * jnp.exp(last - cum),
                jnp.exp(last))

    qg_a, kg_a, qi_a, ki_a, d_a = prepare(rows_a)
    qg_b, kg_b, qi_b, ki_b, d_b = prepare(rows_b)
    v_pair = v_ref[r0:r0 + 2 * GLA_CHUNK, vcols]
    zeros = jnp.zeros((GLA_CHUNK, GLA_DK), BF16)

    yield
    keys_for_a = in_memory_order(kg_a.astype(BF16), kg_b.astype(BF16))
    w_a = _dot_nt(qg_a.astype(BF16), keys_for_a)
    keys_for_b = in_memory_order(jnp.concatenate([zeros, ki_a.astype(BF16)], axis=1),
                                 jnp.concatenate([kg_b.astype(BF16), zeros], axis=1))
    w_b = _dot_nt(jnp.concatenate([qg_b, qi_b], axis=1).astype(BF16), keys_for_b)
    row = lax.broadcasted_iota(jnp.int32, (GLA_CHUNK, 2 * GLA_CHUNK), 0)
    col = lax.broadcasted_iota(jnp.int32, (GLA_CHUNK, 2 * GLA_CHUNK), 1)
    if reverse:
        keep_a = (col >= GLA_CHUNK) & (row < col - GLA_CHUNK)
        keep_b = (col >= GLA_CHUNK) | (row < col)
    else:
        keep_a = (col < GLA_CHUNK) & (row >= col)
        keep_b = (col < GLA_CHUNK) | (row >= col - GLA_CHUNK)
    w_a = jnp.where(keep_a, w_a, 0.0).astype(BF16)
    w_b = jnp.where(keep_b, w_b, 0.0).astype(BF16)

    lhs = jnp.concatenate([in_memory_order(qi_a.astype(BF16), (qi_b * d_a).astype(BF16)),
                           in_memory_order(w_a, w_b)], axis=1)
    last_a = rows_a.start + i_last
    last_b = rows_b.start + i_last
    decay = jnp.exp(ct_ref[kcols, last_a:last_a + 1] + ct_ref[kcols, last_b:last_b + 1])
    keys_to_state = in_memory_order((ki_a * d_b).astype(BF16), ki_b.astype(BF16))

    yield
    state = s_ref[hd]
    rhs = jnp.concatenate([state.astype(BF16), v_pair], axis=0)
    o_ref[r0:r0 + 2 * GLA_CHUNK, vcols] = _dot(lhs, rhs).astype(o_ref.dtype)
    s_ref[hd] = state * decay + _dot_tn(keys_to_state, v_pair)
    yield


def _gla_kernel(qf_ref, kf_ref, cf_ref, vf_ref, qb_ref, kb_ref, cb_ref, vb_ref,
                of_ref, ob_ref, sf_ref, sb_ref, ctf_ref, ctb_ref):
    @pl.when(pl.program_id(1) == 0)
    def _():
        sf_ref[...] = jnp.zeros_like(sf_ref)
        sb_ref[...] = jnp.zeros_like(sb_ref)

    ctf_ref[...] = cf_ref[...].T
    ctb_ref[...] = cb_ref[...].T
    n_pairs = GLA_ROWS // (2 * GLA_CHUNK)
    pairs = []
    for p in range(n_pairs):
        for hd in range(GLA_HEADS):
            pairs.append(_gla_pair(qf_ref, kf_ref, cf_ref, ctf_ref, vf_ref, of_ref, sf_ref, hd, p, reverse=False))
            pairs.append(_gla_pair(qb_ref, kb_ref, cb_ref, ctb_ref, vb_ref, ob_ref, sb_ref, hd, n_pairs - 1 - p,
                                   reverse=True))
    for _stage in range(3):
        for pair in pairs:
            next(pair)


def _gla(main3, cf3, cb3):
    nb, s, _ = main3.shape
    steps = s // GLA_ROWS
    q_col = 2 * D_INNER // GLA_KEY
    k_col = q_col + 1

    def fwd(col):
        return lambda b, i: (b, i, col)

    def bwd(col):
        return lambda b, i: (b, steps - 1 - i, col)

    def specs(idx):
        return [pl.BlockSpec((None, GLA_ROWS, GLA_KEY), idx(q_col)),
                pl.BlockSpec((None, GLA_ROWS, GLA_KEY), idx(k_col)),
                pl.BlockSpec((None, GLA_ROWS, GLA_KEY), idx(0)),
                pl.BlockSpec((None, GLA_ROWS, D_INNER), idx(0))]

    out_sd = jax.ShapeDtypeStruct((nb, s, D_INNER), BF16)
    state = pltpu.VMEM((GLA_HEADS, GLA_DK, GLA_DV), F32)
    cum_t = pltpu.VMEM((GLA_KEY, GLA_ROWS), F32)
    return pl.pallas_call(
        _gla_kernel,
        grid=(nb, steps),
        in_specs=specs(fwd) + specs(bwd),
        out_specs=[pl.BlockSpec((None, GLA_ROWS, D_INNER), fwd(0)),
                   pl.BlockSpec((None, GLA_ROWS, D_INNER), bwd(0))],
        out_shape=[out_sd, out_sd],
        scratch_shapes=[state, state, cum_t, cum_t],
        compiler_params=_cparams(("parallel", "arbitrary")),
        name="gla_scan",
    )(main3, main3, cf3, main3, main3, main3, cb3, main3)


def _attn_finish(acc_ref, l0, l1, lam_ref, og_ref, o_ref, lambda_init):
    lam = lam_ref[...]
    lam_full = (jnp.exp(jnp.sum(lam[0:1] * lam[1:2], axis=1, keepdims=True))
                - jnp.exp(jnp.sum(lam[2:3] * lam[3:4], axis=1, keepdims=True)) + lambda_init)
    o = acc_ref[0] * (1.0 / l0) - lam_full * (acc_ref[1] * (1.0 / l1))
    o_ref[...] = (_rms(o, og_ref[...]) * (1.0 - lambda_init)).astype(o_ref.dtype)


def _lane_select(lane, pieces):
    out = 0.0
    for c, piece in reversed(list(enumerate(pieces))):
        out = jnp.where(lane == c, piece, out)
    return out


def _attn_bounded_kernel(sl_ref, q_ref, k_ref, v_ref, qx_ref, kx_ref, dist_ref, lam_ref, og_ref, o_ref,
                         qa_ref, p_ref, l_ref, acc_ref, *, lambda_init):
    hd = pl.program_id(1)
    qi = pl.program_id(2)
    ratio = TQ // TK
    n_off = k_ref.shape[0] // TK - ratio
    assert ratio == 2 and n_off >= 2 and n_off % 2 == 0
    sl = sl_ref[hd]

    for m in range(2):
        qa_ref[m, :, :DIFF_DQK] = q_ref[:, m * DIFF_DQK:(m + 1) * DIFF_DQK]
        qa_ref[m, :, DIFF_DQK:] = qx_ref[...]
    l_ref[...] = jnp.zeros_like(l_ref)
    acc_ref[...] = jnp.zeros_like(acc_ref)
    lane8 = lax.broadcasted_iota(jnp.int32, (8, LANES), 1)

    def tile_of(t):
        is_left = t < ratio * qi
        return is_left, jnp.where(is_left, t, t + ratio)

    def probabilities(slot, logits_of_map):
        for m in range(2):
            p = jnp.exp2(logits_of_map(m))
            part = p[:, :LANES]
            for c in range(1, TK // LANES):
                part = part + p[:, c * LANES:(c + 1) * LANES]
            l_ref[m] += part
            p_ref[slot, m] = p.astype(BF16)

    def off_diag(t, slot):
        is_left, j = tile_of(t)
        gap = jnp.where(is_left, qi * TQ - (j + 1) * TK, j * TK - (qi + 1) * TQ).astype(F32)
        gap_pieces = [p.astype(F32) for p in _split3(jnp.full((8, LANES), sl * gap, F32))]
        gap_row = _lane_select(lane8, [0.0] * 9 + gap_pieces)[0:1, :].astype(BF16)
        k_ext = kx_ref[jnp.where(is_left, 0, 1)] + gap_row
        k = k_ref[pl.ds(pl.multiple_of(j * TK, TK), TK), :]

        def logits(m):
            ka = jnp.concatenate([k[:, m * DIFF_DQK:(m + 1) * DIFF_DQK], k_ext], axis=1)
            return _dot_nt(qa_ref[m], ka)

        probabilities(slot, logits)

    def diag(d, slot):
        k = k_ref[pl.ds(pl.multiple_of((ratio * qi + d) * TK, TK), TK), :]
        bias = sl * dist_ref[d]

        def logits(m):
            cols = slice(m * DIFF_DQK, (m + 1) * DIFF_DQK)
            return _dot_nt(q_ref[:, cols], k[:, cols]) - bias

        probabilities(slot, logits)

    def weighted_values(j, slot):
        v = v_ref[pl.ds(pl.multiple_of(j * TK, TK), TK), :]
        for m in range(2):
            acc_ref[m] += _dot(p_ref[slot, m], v)

    off_diag(0, 0)

    def pair(u, carry):
        t = 2 * u
        weighted_values(tile_of(t)[1], 0)
        off_diag(t + 1, 1)
        weighted_values(tile_of(t + 1)[1], 1)
        off_diag(t + 2, 0)
        return carry

    lax.fori_loop(0, (n_off - 2) // 2, pair, 0, unroll=7)
    weighted_values(tile_of(n_off - 2)[1], 0)
    off_diag(n_off - 1, 1)
    weighted_values(tile_of(n_off - 1)[1], 1)
    diag(0, 0)
    weighted_values(ratio * qi, 0)
    diag(1, 1)
    weighted_values(ratio * qi + 1, 1)

    _attn_finish(acc_ref, jnp.sum(l_ref[0], axis=1, keepdims=True), jnp.sum(l_ref[1], axis=1, keepdims=True),
                 lam_ref, og_ref, o_ref, lambda_init)


def _split3_np(x):
    x = np.asarray(x, np.float32)
    hi = x.astype(BF16).astype(np.float32)
    mid = (x - hi).astype(BF16).astype(np.float32)
    lo = (x - hi - mid).astype(BF16).astype(np.float32)
    return [hi, mid, lo]


def _alibi_tables(sl_np):
    nh = len(sl_np)
    ii = np.arange(TQ, dtype=np.float32)
    jj = np.arange(TK, dtype=np.float32)
    qx = np.zeros((nh, TQ, LANES), np.float32)
    kx = np.zeros((nh, 2, TK, LANES), np.float32)
    for h, s in enumerate(np.asarray(sl_np, np.float32)):
        qx[h, :, 0:3] = -1.0
        qx[h, :, 9:12] = -1.0
        for c, piece in enumerate(_split3_np(s * ii)):
            qx[h, :, 3 + c] = piece
        for c, piece in enumerate(_split3_np(s * (TQ - ii))):
            qx[h, :, 6 + c] = piece
        for c, piece in enumerate(_split3_np(s * (TK - jj))):
            kx[h, 0, :, c] = piece
        for c, piece in enumerate(_split3_np(s * jj)):
            kx[h, 1, :, c] = piece
        kx[h, 0, :, 3:6] = -1.0
        kx[h, 1, :, 6:9] = -1.0
    return jnp.asarray(qx, BF16), jnp.asarray(kx, BF16)


def _attention_bounded(qkvz3, sl_np, lam, og, lambda_init):
    nb, s, _ = qkvz3.shape
    k_col0 = D_INNER // DIFF_DV
    v_col0 = 2 * k_col0
    qx, kx = _alibi_tables(sl_np)
    i_loc = np.arange(TQ)[None, :, None]
    j_abs = np.arange(TK)[None, None, :] + TK * np.arange(TQ // TK)[:, None, None]
    dist = np.abs(i_loc - j_abs).astype(np.float32)
    return pl.pallas_call(
        functools.partial(_attn_bounded_kernel, lambda_init=lambda_init),
        grid=(nb, DIFF_HEADS, s // TQ),
        in_specs=[pl.BlockSpec(memory_space=pltpu.SMEM),
                  pl.BlockSpec((None, TQ, DIFF_DV), lambda b, h, i: (b, i, h)),
                  pl.BlockSpec((None, s, DIFF_DV), lambda b, h, i: (b, 0, k_col0 + h)),
                  pl.BlockSpec((None, s, DIFF_DV), lambda b, h, i: (b, 0, v_col0 + h)),
                  pl.BlockSpec((None, TQ, LANES), lambda b, h, i: (h, 0, 0)),
                  pl.BlockSpec((None, 2, TK, LANES), lambda b, h, i: (h, 0, 0, 0)),
                  _resident((TQ // TK, TQ, TK)),
                  _resident((4, DIFF_DQK)), _resident((1, DIFF_DV))],
        out_specs=pl.BlockSpec((None, TQ, DIFF_DV), lambda b, h, i: (b, i, h)),
        out_shape=jax.ShapeDtypeStruct((nb, s, D_INNER), BF16),
        scratch_shapes=[pltpu.VMEM((2, TQ, 2 * DIFF_DQK), BF16),
                        pltpu.VMEM((2, 2, TQ, TK), BF16),
                        pltpu.VMEM((2, TQ, LANES), F32),
                        pltpu.VMEM((2, TQ, DIFF_DV), F32)],
        compiler_params=_cparams(("parallel", "parallel", "arbitrary")),
        name="diff_attention_bounded",
    )(jnp.asarray(sl_np, F32), qkvz3, qkvz3, qkvz3, qx, kx, jnp.asarray(dist), lam, og)


def _attn_online_kernel(sl_ref, q_ref, k_ref, v_ref, absd_ref, kx_ref, lam_ref, og_ref, o_ref,
                        ql_ref, qr_ref, m_ref, l_ref, acc_ref, *, lambda_init):
    TQ = TQ_ONLINE
    hd = pl.program_id(1)
    qi = pl.program_id(2)
    n_kv = k_ref.shape[0] // TQ
    slope = sl_ref[hd]

    lane = lax.broadcasted_iota(jnp.int32, (TQ, LANES), 1)
    pieces = [p.astype(F32) for p in _split3(jnp.full((TQ, LANES), slope, F32))]
    ext = _lane_select(lane, pieces + pieces).astype(BF16)
    for m in range(2):
        qm = q_ref[:, m * DIFF_DQK:(m + 1) * DIFF_DQK]
        ql_ref[m, :, :DIFF_DQK] = qm
        ql_ref[m, :, DIFF_DQK:] = ext
        qr_ref[m, :, :DIFF_DQK] = qm
        qr_ref[m, :, DIFF_DQK:] = -ext
    m_ref[...] = jnp.full(m_ref.shape, -1e30, F32)
    l_ref[...] = jnp.zeros_like(l_ref)
    acc_ref[...] = jnp.zeros_like(acc_ref)
    ii = lax.broadcasted_iota(jnp.int32, (TQ, 1), 0).astype(F32)

    def update(m, s, r, v):
        m_old = m_ref[m]
        m_new = jnp.maximum(m_old, jnp.max(s, axis=1, keepdims=True) + r)
        alpha = jnp.exp2(m_old - m_new)
        p = jnp.exp2(s - (m_new - r))
        l_ref[m] = alpha * l_ref[m] + jnp.sum(p, axis=1, keepdims=True)
        acc_ref[m] = alpha * acc_ref[m] + _dot(p.astype(BF16), v)
        m_ref[m] = m_new

    def off_diag(j, qa_ref, r):
        start = pl.multiple_of(j * TQ, TQ)
        k = k_ref[pl.ds(start, TQ), :]
        v = v_ref[pl.ds(start, TQ), :]
        for m in range(2):
            ka = jnp.concatenate([k[:, m * DIFF_DQK:(m + 1) * DIFF_DQK], kx_ref[...]], axis=1)
            update(m, _dot_nt(qa_ref[m], ka), r, v)

    def left(j, carry):
        off_diag(j, ql_ref, -slope * (((qi - j) * TQ).astype(F32) + ii))
        return carry

    def right(j, carry):
        off_diag(j, qr_ref, slope * (ii - ((j - qi) * TQ).astype(F32)))
        return carry

    lax.fori_loop(0, qi, left, 0)

    start = pl.multiple_of(qi * TQ, TQ)
    k = k_ref[pl.ds(start, TQ), :]
    v = v_ref[pl.ds(start, TQ), :]
    bias = -slope * absd_ref[...]
    zero = jnp.zeros((TQ, 1), F32)
    for m in range(2):
        cols = slice(m * DIFF_DQK, (m + 1) * DIFF_DQK)
        update(m, _dot_nt(q_ref[:, cols], k[:, cols]) + bias, zero, v)

    lax.fori_loop(qi + 1, n_kv, right, 0)

    _attn_finish(acc_ref, l_ref[0], l_ref[1], lam_ref, og_ref, o_ref, lambda_init)


def _attention_online(qkvz3, sl, lam, og, lambda_init):
    TQ = TQ_ONLINE
    nb, s, _ = qkvz3.shape
    k_col0 = D_INNER // DIFF_DV
    v_col0 = 2 * k_col0
    pos = np.arange(TQ)
    absd = jnp.asarray(np.abs(pos[:, None] - pos[None, :]), F32)
    kx_np = np.zeros((TQ, LANES), np.float32)
    kx_np[:, 0:3] = ((pos // 16) * 16)[:, None]
    kx_np[:, 3:6] = (pos % 16)[:, None]
    kx = jnp.asarray(kx_np, BF16)
    return pl.pallas_call(
        functools.partial(_attn_online_kernel, lambda_init=lambda_init),
        grid=(nb, DIFF_HEADS, s // TQ),
        in_specs=[pl.BlockSpec(memory_space=pltpu.SMEM),
                  pl.BlockSpec((None, TQ, DIFF_DV), lambda b, h, i: (b, i, h)),
                  pl.BlockSpec((None, s, DIFF_DV), lambda b, h, i: (b, 0, k_col0 + h)),
                  pl.BlockSpec((None, s, DIFF_DV), lambda b, h, i: (b, 0, v_col0 + h)),
                  _resident((TQ, TQ)), _resident((TQ, LANES)),
                  _resident((4, DIFF_DQK)), _resident((1, DIFF_DV))],
        out_specs=pl.BlockSpec((None, TQ, DIFF_DV), lambda b, h, i: (b, i, h)),
        out_shape=jax.ShapeDtypeStruct((nb, s, D_INNER), BF16),
        scratch_shapes=[pltpu.VMEM((2, TQ, 2 * DIFF_DQK), BF16), pltpu.VMEM((2, TQ, 2 * DIFF_DQK), BF16),
                        pltpu.VMEM((2, TQ, 1), F32), pltpu.VMEM((2, TQ, 1), F32),
                        pltpu.VMEM((2, TQ, DIFF_DV), F32)],
        compiler_params=_cparams(("parallel", "parallel", "arbitrary")),
        name="diff_attention_online",
    )(sl, qkvz3, qkvz3, qkvz3, absd, kx, lam, og)


def _prep_a(g, w_in, v_g, w_s, b_s, w_out):
    return (g[None], w_in.astype(BF16), v_g[None], w_s.astype(BF16), b_s[:, :, None], w_out.astype(BF16))


def _prep_b(g, w_in, w_gate, gate_bias, o_g, w_out):
    c_q, c_k, c_v, c_g, c_a = 0, GLA_KEY, 2 * GLA_KEY, 2 * GLA_KEY + D_INNER, 2 * GLA_KEY + 2 * D_INNER
    w_main = jnp.concatenate([w_in[:, c_v:c_g], w_in[:, c_g:c_a], w_in[:, c_q:c_k], w_in[:, c_k:c_v]],
                             axis=1).astype(BF16)
    w_a = jnp.pad(w_in[:, c_a:], ((0, 0), (0, LANES - 2 * GLA_RANK))).astype(BF16)
    wg = jnp.zeros((LANES, 2 * GLA_KEY), F32)
    wg = wg.at[:GLA_RANK, :GLA_KEY].set(w_gate[0]).at[GLA_RANK:2 * GLA_RANK, GLA_KEY:].set(w_gate[1])
    rows = np.arange(MXU_DEPTH)
    same_chunk = (rows[:, None] // GLA_CHUNK) == (rows[None, :] // GLA_CHUNK)
    tril = jnp.asarray(same_chunk & (rows[None, :] <= rows[:, None]), BF16)
    triu = jnp.asarray(same_chunk & (rows[None, :] >= rows[:, None]), BF16)
    return (g[None], w_main, w_a, wg.astype(BF16), gate_bias.reshape(1, -1), tril, triu, o_g[None],
            w_out.astype(BF16))


def _layer_b(x, g, w_main, w_a, wg, gate_bias, tril, triu, o_g, w_out):
    nb = x.shape[0] // SEQ
    main, cf, cb = _inproj_b(x, g, w_main, w_a, wg, gate_bias, tril, triu)
    o_f, o_b = _gla(main.reshape(nb, SEQ, -1), cf.reshape(nb, SEQ, -1), cb.reshape(nb, SEQ, -1))
    return _outproj_b(o_f.reshape(-1, D_INNER), o_b.reshape(-1, D_INNER), main, o_g, w_out, x)


def _prep_c(g, w_in, q_g, k_g, lam, o_g, w_out, lambda_init):
    logit_bound = jnp.max(jnp.abs(q_g)) * jnp.max(jnp.abs(k_g)) * (DIFF_DQK ** 0.5 * LOG2E * 1.02)
    return (g[None], w_in.astype(BF16), q_g[None], k_g[None], lam, o_g[None], w_out.astype(BF16),
            logit_bound <= MAX_UNSHIFTED_LOG2, lambda_init)


def _layer_c(x, g, w_in, q_g, k_g, lam, o_g, w_out, logits_bounded, lambda_init):
    nb = x.shape[0] // SEQ
    qkvz = _inproj_c(x, g, w_in, q_g, k_g)
    sl_np = (2.0 ** (-8.0 * np.arange(1, DIFF_HEADS + 1) / DIFF_HEADS) * LOG2E).astype(np.float32)
    qkvz3 = qkvz.reshape(nb, SEQ, -1)
    o = lax.cond(logits_bounded,
                 lambda: _attention_bounded(qkvz3, sl_np, lam, o_g, lambda_init),
                 lambda: _attention_online(qkvz3, jnp.asarray(sl_np), lam, o_g, lambda_init))
    return _outproj_c(o.reshape(-1, D_INNER), qkvz, w_out, x)


def kernel(x_prompt, x_sample, norm_g, a_w_in, a_v_g, a_w_s, a_b_s, a_w_out, b_w_in, b_w_gate, b_gate_bias,
           b_o_g, b_w_out, c_w_in, c_q_g, c_k_g, c_lam, c_o_g, c_w_out):
    layers = []
    for i in range(DEPTH):
        kind, j = i % 3, i // 3
        if kind == 0:
            layers.append((_layer_a_call, _prep_a(norm_g[i], a_w_in[j], a_v_g[j], a_w_s[j], a_b_s[j], a_w_out[j])))
        elif kind == 1:
            layers.append((_layer_b, _prep_b(norm_g[i], b_w_in[j], b_w_gate[j], b_gate_bias[j], b_o_g[j],
                                             b_w_out[j])))
        else:
            lambda_init = 0.8 - 0.6 * math.exp(-0.3 * i)
            layers.append((_layer_c, _prep_c(norm_g[i], c_w_in[j], c_q_g[j], c_k_g[j], c_lam[j], c_o_g[j],
                                             c_w_out[j], lambda_init)))

    def trunk(x_group):
        x = x_group.reshape(-1, D_MODEL)
        for layer, params in layers:
            x = layer(x, *params)
        return x.reshape(x_group.shape)

    return trunk(x_prompt), trunk(x_sample)
```

```python
import functools
import math

import jax
import jax.numpy as jnp
import numpy as np
from jax import lax
from jax.experimental import pallas as pl
from jax.experimental.pallas import tpu as pltpu

F32 = jnp.float32
BF16 = jnp.bfloat16

D_MODEL = 1024
SEQ = 16384
DEPTH = 4
D_INNER = 2048
EPS = 1e-6

SG_CHUNK = 128
SG_GROUPS = 8
SG_GDIM = D_INNER // SG_GROUPS

GLA_HEADS = 4
GLA_KEY = 512
GLA_DK = 128
GLA_DV = 512
GLA_RANK = 16
GLA_TAU = 16.0
GLA_CHUNK = 64

DIFF_HEADS = 8
DIFF_DQK = 128
DIFF_DV = 256

LANES = 128
MXU_DEPTH = 256
VMEM_LIMIT = 56 * 1024 * 1024

TM = 512
RING_SLOTS = 3
GLA_ROWS = 512
TQ = 1024
TK = 512
TQ_ONLINE = 512
LOG2E = math.log2(math.e)
MAX_UNSHIFTED_LOG2 = 80.0


def _cparams(sem):
    return pltpu.CompilerParams(dimension_semantics=sem, vmem_limit_bytes=VMEM_LIMIT)


def _resident(shape):
    nd = len(shape)
    return pl.BlockSpec(shape, lambda *_: (0,) * nd, pipeline_mode=pl.Buffered(1))


def _rms(x, g):
    return x * lax.rsqrt(jnp.mean(x * x, axis=-1, keepdims=True) + EPS) * g


def _silu(z):
    return z * jax.nn.sigmoid(z)


def _dot(a, b):
    return jnp.dot(a, b, preferred_element_type=F32)


def _dot_nt(a, b):
    return lax.dot_general(a, b, (((1,), (1,)), ((), ())), preferred_element_type=F32)


def _dot_tn(a, b):
    return lax.dot_general(a, b, (((0,), (0,)), ((), ())), preferred_element_type=F32)


def _layer_a_kernel(x_ref, g_ref, w_in_ref, vg_ref, ws_ref, bs_ref, w_out_ref, o_ref, uvz_ref, y_ref):
    x = x_ref[...]
    h = _rms(x, g_ref[...]).astype(BF16)
    for blk in range(3):
        lo = blk * D_INNER
        y = _dot(h, w_in_ref[:, lo:lo + D_INNER])
        if blk == 1:
            y = _rms(y, vg_ref[...])
        uvz_ref[:, lo:lo + D_INNER] = y.astype(BF16)
    for c in range(TM // SG_CHUNK):
        rows = slice(c * SG_CHUNK, (c + 1) * SG_CHUNK)
        for grp in range(SG_GROUPS):
            u_cols = slice(grp * SG_GDIM, (grp + 1) * SG_GDIM)
            v_cols = slice(D_INNER + u_cols.start, D_INNER + u_cols.stop)
            z_cols = slice(2 * D_INNER + u_cols.start, 2 * D_INNER + u_cols.stop)
            sv = _dot(ws_ref[grp], uvz_ref[rows, v_cols]) + bs_ref[grp]
            y = uvz_ref[rows, u_cols].astype(F32) * sv * _silu(uvz_ref[rows, z_cols].astype(F32))
            y_ref[rows, u_cols] = y.astype(BF16)
    o_ref[...] = x + _dot(y_ref[...], w_out_ref[...])


def _layer_a_call(x, g, w_in, vg, ws, bs, w_out):
    t = x.shape[0]
    n = 3 * D_INNER
    return pl.pallas_call(
        _layer_a_kernel,
        grid=(t // TM,),
        in_specs=[pl.BlockSpec((TM, D_MODEL), lambda i: (i, 0)),
                  _resident((1, D_MODEL)), _resident((D_MODEL, n)), _resident((1, D_INNER)),
                  _resident((SG_GROUPS, SG_CHUNK, SG_CHUNK)), _resident((SG_GROUPS, SG_CHUNK, 1)),
                  _resident((D_INNER, D_MODEL))],
        out_specs=pl.BlockSpec((TM, D_MODEL), lambda i: (i, 0)),
        out_shape=jax.ShapeDtypeStruct((t, D_MODEL), F32),
        scratch_shapes=[pltpu.VMEM((TM, n), BF16), pltpu.VMEM((TM, D_INNER), BF16)],
        compiler_params=_cparams(("parallel",)),
        name="layer_a",
    )(x, g, w_in, vg, ws, bs, w_out)


def _log_sigmoid(x):
    return -(jnp.maximum(-x, 0.0) + jnp.log1p(jnp.exp(-jnp.abs(x))))


def _split3(x):
    hi = x.astype(BF16)
    r1 = x - hi.astype(F32)
    mid = r1.astype(BF16)
    lo = (r1 - mid.astype(F32)).astype(BF16)
    return hi, mid, lo


def _inproj_b_kernel(x_ref, g_ref, w_ref, wa_ref, wg_ref, gb_ref, tril_ref, triu_ref,
                     o_ref, cf_ref, cb_ref):
    h = _rms(x_ref[...], g_ref[...]).astype(BF16)
    code = _dot(h, wa_ref[...]).astype(BF16)
    la = _log_sigmoid(_dot(code, wg_ref[...]) + gb_ref[...]) / GLA_TAU
    pieces_f = _split3(la[:, :GLA_KEY])[:2]
    pieces_b = _split3(la[:, GLA_KEY:])[:2]
    for blk in range(2):
        lo = blk * D_INNER
        o_ref[:, lo:lo + D_INNER] = _dot(h, w_ref[:, lo:lo + D_INNER]).astype(BF16)
    lo = 2 * D_INNER
    qk = _dot(h, w_ref[:, lo:lo + 2 * GLA_KEY])
    o_ref[:, lo:lo + GLA_KEY] = (qk[:, :GLA_KEY] * GLA_DK ** -0.5).astype(BF16)
    o_ref[:, lo + GLA_KEY:lo + 2 * GLA_KEY] = qk[:, GLA_KEY:].astype(BF16)
    for blk in range(TM // MXU_DEPTH):
        rows = slice(blk * MXU_DEPTH, (blk + 1) * MXU_DEPTH)
        cf_ref[rows, :] = sum(_dot(tril_ref[...], piece[rows, :]) for piece in pieces_f)
        cb_ref[rows, :] = sum(_dot(triu_ref[...], piece[rows, :]) for piece in pieces_b)


def _inproj_b(x, g, w_main, w_a, w_gate, gate_bias, tril, triu):
    t = x.shape[0]
    n = w_main.shape[1]
    return pl.pallas_call(
        _inproj_b_kernel,
        grid=(t // TM,),
        in_specs=[pl.BlockSpec((TM, D_MODEL), lambda i: (i, 0)),
                  _resident((1, D_MODEL)), _resident((D_MODEL, n)), _resident((D_MODEL, LANES)),
                  _resident((LANES, 2 * GLA_KEY)), _resident((1, 2 * GLA_KEY)),
                  _resident((MXU_DEPTH, MXU_DEPTH)), _resident((MXU_DEPTH, MXU_DEPTH))],
        out_specs=[pl.BlockSpec((TM, n), lambda i: (i, 0)),
                   pl.BlockSpec((TM, GLA_KEY), lambda i: (i, 0)),
                   pl.BlockSpec((TM, GLA_KEY), lambda i: (i, 0))],
        out_shape=[jax.ShapeDtypeStruct((t, n), BF16),
                   jax.ShapeDtypeStruct((t, GLA_KEY), F32),
                   jax.ShapeDtypeStruct((t, GLA_KEY), F32)],
        compiler_params=_cparams(("parallel",)),
        name="inproj_b",
    )(x, g, w_main, w_a, w_gate, gate_bias, tril, triu)


def _inproj_c_kernel(x_ref, g_ref, w_ref, qg_ref, kg_ref, o_ref):
    h = _rms(x_ref[...], g_ref[...]).astype(BF16)
    half = D_INNER // 2
    for blk in range(8):
        lo = blk * half
        y = _dot(h, w_ref[:, lo:lo + half])
        if blk < 4:
            gain = qg_ref[...] if blk < 2 else kg_ref[...]
            scale = DIFF_DQK ** -0.5 * LOG2E if blk < 2 else 1.0
            for grp in range(half // DIFF_DQK):
                sl = slice(grp * DIFF_DQK, (grp + 1) * DIFF_DQK)
                o_ref[:, lo + sl.start:lo + sl.stop] = (_rms(y[:, sl], gain) * scale).astype(BF16)
        else:
            o_ref[:, lo:lo + half] = y.astype(BF16)


def _inproj_c(x, g, w, qg, kg):
    t = x.shape[0]
    n = 4 * D_INNER
    return pl.pallas_call(
        _inproj_c_kernel,
        grid=(t // TM,),
        in_specs=[pl.BlockSpec((TM, D_MODEL), lambda i: (i, 0)),
                  _resident((1, D_MODEL)), _resident((D_MODEL, n)),
                  _resident((1, DIFF_DQK)), _resident((1, DIFF_DQK))],
        out_specs=pl.BlockSpec((TM, n), lambda i: (i, 0)),
        out_shape=jax.ShapeDtypeStruct((t, n), BF16),
        compiler_params=_cparams(("parallel",)),
        name="inproj_c",
    )(x, g, w, qg, kg)


def _ring_fetch(streams, bufs, sem):
    step = pl.program_id(0)
    n_steps = pl.num_programs(0)

    def copies(s):
        slot = s % RING_SLOTS
        rows = pl.ds(pl.multiple_of(s * TM, TM), TM)
        return [pltpu.make_async_copy(hbm.at[rows, pl.ds(col0, buf.shape[2])], buf.at[slot], sem.at[k, slot])
                for k, ((hbm, col0), buf) in enumerate(zip(streams, bufs))]

    @pl.when(step == 0)
    def _():
        for first in range(RING_SLOTS - 1):
            for copy in copies(first):
                copy.start()

    @pl.when(step + RING_SLOTS - 1 < n_steps)
    def _():
        for copy in copies(step + RING_SLOTS - 1):
            copy.start()

    for copy in copies(step):
        copy.wait()
    return step % RING_SLOTS


def _ring_call(kernel, streams, residents, t, name):
    assert t // TM >= RING_SLOTS
    scratch = [pltpu.VMEM((RING_SLOTS, TM, width), arr.dtype) for arr, _, width in streams]
    scratch.append(pltpu.SemaphoreType.DMA((len(streams), RING_SLOTS)))
    return pl.pallas_call(
        functools.partial(kernel, cols=tuple(col0 for _, col0, _ in streams)),
        grid=(t // TM,),
        in_specs=[pl.BlockSpec(memory_space=pl.ANY)] * len(streams) + [_resident(r.shape) for r in residents],
        out_specs=pl.BlockSpec((TM, D_MODEL), lambda i: (i, 0)),
        out_shape=jax.ShapeDtypeStruct((t, D_MODEL), F32),
        scratch_shapes=scratch,
        compiler_params=_cparams(("arbitrary",)),
        name=name,
    )(*[arr for arr, _, _ in streams], *residents)


def _outproj_b_kernel(of_hbm, ob_hbm, g_hbm, x_hbm, og_ref, w_ref, o_ref, of_buf, ob_buf, g_buf, x_buf, sem, *,
                      cols):
    slot = _ring_fetch(list(zip((of_hbm, ob_hbm, g_hbm, x_hbm), cols)), (of_buf, ob_buf, g_buf, x_buf), sem)
    out = x_buf[slot]
    for hd in range(GLA_HEADS):
        cols_h = slice(hd * GLA_DV, (hd + 1) * GLA_DV)
        o = of_buf[slot, :, cols_h].astype(F32) + ob_buf[slot, :, cols_h].astype(F32)
        y = _rms(o, og_ref[...]) * _silu(g_buf[slot, :, cols_h].astype(F32))
        out = out + _dot(y.astype(BF16), w_ref[cols_h, :])
    o_ref[...] = out


def _outproj_b(o_f, o_b, main, og, w, x):
    streams = [(o_f, 0, D_INNER), (o_b, 0, D_INNER), (main, D_INNER, D_INNER), (x, 0, D_MODEL)]
    return _ring_call(_outproj_b_kernel, streams, [og, w], x.shape[0], "outproj_b")


def _outproj_c_kernel(a_hbm, z_hbm, x_hbm, w_ref, o_ref, a_buf, z_buf, x_buf, sem, *, cols):
    slot = _ring_fetch(list(zip((a_hbm, z_hbm, x_hbm), cols)), (a_buf, z_buf, x_buf), sem)
    out = x_buf[slot]
    quarter = D_INNER // 4
    for part in range(4):
        cols_p = slice(part * quarter, (part + 1) * quarter)
        y = (a_buf[slot, :, cols_p].astype(F32) * _silu(z_buf[slot, :, cols_p].astype(F32))).astype(BF16)
        out = out + _dot(y, w_ref[cols_p, :])
    o_ref[...] = out


def _outproj_c(o, qkvz, w, x):
    streams = [(o, 0, D_INNER), (qkvz, 3 * D_INNER, D_INNER), (x, 0, D_MODEL)]
    return _ring_call(_outproj_c_kernel, streams, [w], x.shape[0], "outproj_c")


def _gla_pair(q_ref, k_ref, c_ref, ct_ref, v_ref, o_ref, s_ref, hd, pair, reverse):
    i_ref, i_last = (GLA_CHUNK // 2 - 1, 0) if reverse else (GLA_CHUNK // 2, GLA_CHUNK - 1)
    kcols = slice(hd * GLA_DK, (hd + 1) * GLA_DK)
    vcols = slice(hd * GLA_DV, (hd + 1) * GLA_DV)
    r0 = pair * 2 * GLA_CHUNK
    first, second = slice(r0, r0 + GLA_CHUNK), slice(r0 + GLA_CHUNK, r0 + 2 * GLA_CHUNK)
    rows_a, rows_b = (second, first) if reverse else (first, second)

    def in_memory_order(of_a, of_b):
        return jnp.concatenate([of_b, of_a] if reverse else [of_a, of_b], axis=0)

    def prepare(rows):
        q = q_ref[rows, kcols].astype(F32)
        k = k_ref[rows, kcols].astype(F32)
        cum = c_ref[rows, kcols]
        ref = cum[i_ref:i_ref + 1, :]
        last = cum[i_last:i_last + 1, :]
        return (q * jnp.exp(cum - ref), k * jnp.exp(ref - cum), q * jnp.exp(cum), k * jnp.exp(last - cum),
                jnp.exp(last))

    qg_a, kg_a, qi_a, ki_a, d_a = prepare(rows_a)
    qg_b, kg_b, qi_b, ki_b, d_b = prepare(rows_b)
    v_pair = v_ref[r0:r0 + 2 * GLA_CHUNK, vcols]
    zeros = jnp.zeros((GLA_CHUNK, GLA_DK), BF16)

    yield
    keys_for_a = in_memory_order(kg_a.astype(BF16), kg_b.astype(BF16))
    w_a = _dot_nt(qg_a.astype(BF16), keys_for_a)
    keys_for_b = in_memory_order(jnp.concatenate([zeros, ki_a.astype(BF16)], axis=1),
                                 jnp.concatenate([kg_b.astype(BF16), zeros], axis=1))
    w_b = _dot_nt(jnp.concatenate([qg_b, qi_b], axis=1).astype(BF16), keys_for_b)
    row = lax.broadcasted_iota(jnp.int32, (GLA_CHUNK, 2 * GLA_CHUNK), 0)
    col = lax.broadcasted_iota(jnp.int32, (GLA_CHUNK, 2 * GLA_CHUNK), 1)
    if reverse:
        keep_a = (col >= GLA_CHUNK) & (row < col - GLA_CHUNK)
        keep_b = (col >= GLA_CHUNK) | (row < col)
    else:
        keep_a = (col < GLA_CHUNK) & (row >= col)
        keep_b = (col < GLA_CHUNK) | (row >= col - GLA_CHUNK)
    w_a = jnp.where(keep_a, w_a, 0.0).astype(BF16)
    w_b = jnp.where(keep_b, w_b, 0.0).astype(BF16)

    lhs = jnp.concatenate([in_memory_order(qi_a.astype(BF16), (qi_b * d_a).astype(BF16)),
                           in_memory_order(w_a, w_b)], axis=1)
    last_a = rows_a.start + i_last
    last_b = rows_b.start + i_last
    decay = jnp.exp(ct_ref[kcols, last_a:last_a + 1] + ct_ref[kcols, last_b:last_b + 1])
    keys_to_state = in_memory_order((ki_a * d_b).astype(BF16), ki_b.astype(BF16))

    yield
    state = s_ref[hd]
    rhs = jnp.concatenate([state.astype(BF16), v_pair], axis=0)
    o_ref[r0:r0 + 2 * GLA_CHUNK, vcols] = _dot(lhs, rhs).astype(o_ref.dtype)
    s_ref[hd] = state * decay + _dot_tn(keys_to_state, v_pair)
    yield


def _gla_kernel(qf_ref, kf_ref, cf_ref, vf_ref, qb_ref, kb_ref, cb_ref, vb_ref,
                of_ref, ob_ref, sf_ref, sb_ref, ctf_ref, ctb_ref):
    @pl.when(pl.program_id(1) == 0)
    def _():
        sf_ref[...] = jnp.zeros_like(sf_ref)
        sb_ref[...] = jnp.zeros_like(sb_ref)

    ctf_ref[...] = cf_ref[...].T
    ctb_ref[...] = cb_ref[...].T
    n_pairs = GLA_ROWS // (2 * GLA_CHUNK)
    pairs = []
    for p in range(n_pairs):
        for hd in range(GLA_HEADS):
            pairs.append(_gla_pair(qf_ref, kf_ref, cf_ref, ctf_ref, vf_ref, of_ref, sf_ref, hd, p, reverse=False))
            pairs.append(_gla_pair(qb_ref, kb_ref, cb_ref, ctb_ref, vb_ref, ob_ref, sb_ref, hd, n_pairs - 1 - p,
                                   reverse=True))
    for _stage in range(3):
        for pair in pairs:
            next(pair)


def _gla(main3, cf3, cb3):
    nb, s, _ = main3.shape
    steps = s // GLA_ROWS
    q_col = 2 * D_INNER // GLA_KEY
    k_col = q_col + 1

    def fwd(col):
        return lambda b, i: (b, i, col)

    def bwd(col):
        return lambda b, i: (b, steps - 1 - i, col)

    def specs(idx):
        return [pl.BlockSpec((None, GLA_ROWS, GLA_KEY), idx(q_col)),
                pl.BlockSpec((None, GLA_ROWS, GLA_KEY), idx(k_col)),
                pl.BlockSpec((None, GLA_ROWS, GLA_KEY), idx(0)),
                pl.BlockSpec((None, GLA_ROWS, D_INNER), idx(0))]

    out_sd = jax.ShapeDtypeStruct((nb, s, D_INNER), BF16)
    state = pltpu.VMEM((GLA_HEADS, GLA_DK, GLA_DV), F32)
    cum_t = pltpu.VMEM((GLA_KEY, GLA_ROWS), F32)
    return pl.pallas_call(
        _gla_kernel,
        grid=(nb, steps),
        in_specs=specs(fwd) + specs(bwd),
        out_specs=[pl.BlockSpec((None, GLA_ROWS, D_INNER), fwd(0)),
                   pl.BlockSpec((None, GLA_ROWS, D_INNER), bwd(0))],
        out_shape=[out_sd, out_sd],
        scratch_shapes=[state, state, cum_t, cum_t],
        compiler_params=_cparams(("parallel", "arbitrary")),
        name="gla_scan",
    )(main3, main3, cf3, main3, main3, main3, cb3, main3)


def _attn_finish(acc_ref, l0, l1, lam_ref, og_ref, o_ref, lambda_init):
    lam = lam_ref[...]
    lam_full = (jnp.exp(jnp.sum(lam[0:1] * lam[1:2], axis=1, keepdims=True))
                - jnp.exp(jnp.sum(lam[2:3] * lam[3:4], axis=1, keepdims=True)) + lambda_init)
    o = acc_ref[0] * (1.0 / l0) - lam_full * (acc_ref[1] * (1.0 / l1))
    o_ref[...] = (_rms(o, og_ref[...]) * (1.0 - lambda_init)).astype(o_ref.dtype)


def _lane_select(lane, pieces):
    out = 0.0
    for c, piece in reversed(list(enumerate(pieces))):
        out = jnp.where(lane == c, piece, out)
    return out


def _attn_bounded_kernel(sl_ref, q_ref, k_ref, v_ref, qx_ref, kx_ref, dist_ref, lam_ref, og_ref, o_ref,
                         qa_ref, p_ref, l_ref, acc_ref, *, lambda_init):
    hd = pl.program_id(1)
    qi = pl.program_id(2)
    ratio = TQ // TK
    n_off = k_ref.shape[0] // TK - ratio
    assert ratio == 2 and n_off >= 2 and n_off % 2 == 0
    sl = sl_ref[hd]

    for m in range(2):
        qa_ref[m, :, :DIFF_DQK] = q_ref[:, m * DIFF_DQK:(m + 1) * DIFF_DQK]
        qa_ref[m, :, DIFF_DQK:] = qx_ref[...]
    l_ref[...] = jnp.zeros_like(l_ref)
    acc_ref[...] = jnp.zeros_like(acc_ref)
    lane8 = lax.broadcasted_iota(jnp.int32, (8, LANES), 1)

    def tile_of(t):
        is_left = t < ratio * qi
        return is_left, jnp.where(is_left, t, t + ratio)

    def probabilities(slot, logits_of_map):
        for m in range(2):
            p = jnp.exp2(logits_of_map(m))
            part = p[:, :LANES]
            for c in range(1, TK // LANES):
                part = part + p[:, c * LANES:(c + 1) * LANES]
            l_ref[m] += part
            p_ref[slot, m] = p.astype(BF16)

    def off_diag(t, slot):
        is_left, j = tile_of(t)
        gap = jnp.where(is_left, qi * TQ - (j + 1) * TK, j * TK - (qi + 1) * TQ).astype(F32)
        gap_pieces = [p.astype(F32) for p in _split3(jnp.full((8, LANES), sl * gap, F32))]
        gap_row = _lane_select(lane8, [0.0] * 9 + gap_pieces)[0:1, :].astype(BF16)
        k_ext = kx_ref[jnp.where(is_left, 0, 1)] + gap_row
        k = k_ref[pl.ds(pl.multiple_of(j * TK, TK), TK), :]

        def logits(m):
            ka = jnp.concatenate([k[:, m * DIFF_DQK:(m + 1) * DIFF_DQK], k_ext], axis=1)
            return _dot_nt(qa_ref[m], ka)

        probabilities(slot, logits)

    def diag(d, slot):
        k = k_ref[pl.ds(pl.multiple_of((ratio * qi + d) * TK, TK), TK), :]
        bias = sl * dist_ref[d]

        def logits(m):
            cols = slice(m * DIFF_DQK, (m + 1) * DIFF_DQK)
            return _dot_nt(q_ref[:, cols], k[:, cols]) - bias

        probabilities(slot, logits)

    def weighted_values(j, slot):
        v = v_ref[pl.ds(pl.multiple_of(j * TK, TK), TK), :]
        for m in range(2):
            acc_ref[m] += _dot(p_ref[slot, m], v)

    off_diag(0, 0)

    def pair(u, carry):
        t = 2 * u
        weighted_values(tile_of(t)[1], 0)
        off_diag(t + 1, 1)
        weighted_values(tile_of(t + 1)[1], 1)
        off_diag(t + 2, 0)
        return carry

    lax.fori_loop(0, (n_off - 2) // 2, pair, 0, unroll=7)
    weighted_values(tile_of(n_off - 2)[1], 0)
    off_diag(n_off - 1, 1)
    weighted_values(tile_of(n_off - 1)[1], 1)
    diag(0, 0)
    weighted_values(ratio * qi, 0)
    diag(1, 1)
    weighted_values(ratio * qi + 1, 1)

    _attn_finish(acc_ref, jnp.sum(l_ref[0], axis=1, keepdims=True), jnp.sum(l_ref[1], axis=1, keepdims=True),
                 lam_ref, og_ref, o_ref, lambda_init)


def _split3_np(x):
    x = np.asarray(x, np.float32)
    hi = x.astype(BF16).astype(np.float32)
    mid = (x - hi).astype(BF16).astype(np.float32)
    lo = (x - hi - mid).astype(BF16).astype(np.float32)
    return [hi, mid, lo]


def _alibi_tables(sl_np):
    nh = len(sl_np)
    ii = np.arange(TQ, dtype=np.float32)
    jj = np.arange(TK, dtype=np.float32)
    qx = np.zeros((nh, TQ, LANES), np.float32)
    kx = np.zeros((nh, 2, TK, LANES), np.float32)
    for h, s in enumerate(np.asarray(sl_np, np.float32)):
        qx[h, :, 0:3] = -1.0
        qx[h, :, 9:12] = -1.0
        for c, piece in enumerate(_split3_np(s * ii)):
            qx[h, :, 3 + c] = piece
        for c, piece in enumerate(_split3_np(s * (TQ - ii))):
            qx[h, :, 6 + c] = piece
        for c, piece in enumerate(_split3_np(s * (TK - jj))):
            kx[h, 0, :, c] = piece
        for c, piece in enumerate(_split3_np(s * jj)):
            kx[h, 1, :, c] = piece
        kx[h, 0, :, 3:6] = -1.0
        kx[h, 1, :, 6:9] = -1.0
    return jnp.asarray(qx, BF16), jnp.asarray(kx, BF16)


def _attention_bounded(qkvz3, sl_np, lam, og, lambda_init):
    nb, s, _ = qkvz3.shape
    k_col0 = D_INNER // DIFF_DV
    v_col0 = 2 * k_col0
    qx, kx = _alibi_tables(sl_np)
    i_loc = np.arange(TQ)[None, :, None]
    j_abs = np.arange(TK)[None, None, :] + TK * np.arange(TQ // TK)[:, None, None]
    dist = np.abs(i_loc - j_abs).astype(np.float32)
    return pl.pallas_call(
        functools.partial(_attn_bounded_kernel, lambda_init=lambda_init),
        grid=(nb, DIFF_HEADS, s // TQ),
        in_specs=[pl.BlockSpec(memory_space=pltpu.SMEM),
                  pl.BlockSpec((None, TQ, DIFF_DV), lambda b, h, i: (b, i, h)),
                  pl.BlockSpec((None, s, DIFF_DV), lambda b, h, i: (b, 0, k_col0 + h)),
                  pl.BlockSpec((None, s, DIFF_DV), lambda b, h, i: (b, 0, v_col0 + h)),
                  pl.BlockSpec((None, TQ, LANES), lambda b, h, i: (h, 0, 0)),
                  pl.BlockSpec((None, 2, TK, LANES), lambda b, h, i: (h, 0, 0, 0)),
                  _resident((TQ // TK, TQ, TK)),
                  _resident((4, DIFF_DQK)), _resident((1, DIFF_DV))],
        out_specs=pl.BlockSpec((None, TQ, DIFF_DV), lambda b, h, i: (b, i, h)),
        out_shape=jax.ShapeDtypeStruct((nb, s, D_INNER), BF16),
        scratch_shapes=[pltpu.VMEM((2, TQ, 2 * DIFF_DQK), BF16),
                        pltpu.VMEM((2, 2, TQ, TK), BF16),
                        pltpu.VMEM((2, TQ, LANES), F32),
                        pltpu.VMEM((2, TQ, DIFF_DV), F32)],
        compiler_params=_cparams(("parallel", "parallel", "arbitrary")),
        name="diff_attention_bounded",
    )(jnp.asarray(sl_np, F32), qkvz3, qkvz3, qkvz3, qx, kx, jnp.asarray(dist), lam, og)


def _attn_online_kernel(sl_ref, q_ref, k_ref, v_ref, absd_ref, kx_ref, lam_ref, og_ref, o_ref,
                        ql_ref, qr_ref, m_ref, l_ref, acc_ref, *, lambda_init):
    TQ = TQ_ONLINE
    hd = pl.program_id(1)
    qi = pl.program_id(2)
    n_kv = k_ref.shape[0] // TQ
    slope = sl_ref[hd]

    lane = lax.broadcasted_iota(jnp.int32, (TQ, LANES), 1)
    pieces = [p.astype(F32) for p in _split3(jnp.full((TQ, LANES), slope, F32))]
    ext = _lane_select(lane, pieces + pieces).astype(BF16)
    for m in range(2):
        qm = q_ref[:, m * DIFF_DQK:(m + 1) * DIFF_DQK]
        ql_ref[m, :, :DIFF_DQK] = qm
        ql_ref[m, :, DIFF_DQK:] = ext
        qr_ref[m, :, :DIFF_DQK] = qm
        qr_ref[m, :, DIFF_DQK:] = -ext
    m_ref[...] = jnp.full(m_ref.shape, -1e30, F32)
    l_ref[...] = jnp.zeros_like(l_ref)
    acc_ref[...] = jnp.zeros_like(acc_ref)
    ii = lax.broadcasted_iota(jnp.int32, (TQ, 1), 0).astype(F32)

    def update(m, s, r, v):
        m_old = m_ref[m]
        m_new = jnp.maximum(m_old, jnp.max(s, axis=1, keepdims=True) + r)
        alpha = jnp.exp2(m_old - m_new)
        p = jnp.exp2(s - (m_new - r))
        l_ref[m] = alpha * l_ref[m] + jnp.sum(p, axis=1, keepdims=True)
        acc_ref[m] = alpha * acc_ref[m] + _dot(p.astype(BF16), v)
        m_ref[m] = m_new

    def off_diag(j, qa_ref, r):
        start = pl.multiple_of(j * TQ, TQ)
        k = k_ref[pl.ds(start, TQ), :]
        v = v_ref[pl.ds(start, TQ), :]
        for m in range(2):
            ka = jnp.concatenate([k[:, m * DIFF_DQK:(m + 1) * DIFF_DQK], kx_ref[...]], axis=1)
            update(m, _dot_nt(qa_ref[m], ka), r, v)

    def left(j, carry):
        off_diag(j, ql_ref, -slope * (((qi - j) * TQ).astype(F32) + ii))
        return carry

    def right(j, carry):
        off_diag(j, qr_ref, slope * (ii - ((j - qi) * TQ).astype(F32)))
        return carry

    lax.fori_loop(0, qi, left, 0)

    start = pl.multiple_of(qi * TQ, TQ)
    k = k_ref[pl.ds(start, TQ), :]
    v = v_ref[pl.ds(start, TQ), :]
    bias = -slope * absd_ref[...]
    zero = jnp.zeros((TQ, 1), F32)
    for m in range(2):
        cols = slice(m * DIFF_DQK, (m + 1) * DIFF_DQK)
        update(m, _dot_nt(q_ref[:, cols], k[:, cols]) + bias, zero, v)

    lax.fori_loop(qi + 1, n_kv, right, 0)

    _attn_finish(acc_ref, l_ref[0], l_ref[1], lam_ref, og_ref, o_ref, lambda_init)


def _attention_online(qkvz3, sl, lam, og, lambda_init):
    TQ = TQ_ONLINE
    nb, s, _ = qkvz3.shape
    k_col0 = D_INNER // DIFF_DV
    v_col0 = 2 * k_col0
    pos = np.arange(TQ)
    absd = jnp.asarray(np.abs(pos[:, None] - pos[None, :]), F32)
    kx_np = np.zeros((TQ, LANES), np.float32)
    kx_np[:, 0:3] = ((pos // 16) * 16)[:, None]
    kx_np[:, 3:6] = (pos % 16)[:, None]
    kx = jnp.asarray(kx_np, BF16)
    return pl.pallas_call(
        functools.partial(_attn_online_kernel, lambda_init=lambda_init),
        grid=(nb, DIFF_HEADS, s // TQ),
        in_specs=[pl.BlockSpec(memory_space=pltpu.SMEM),
                  pl.BlockSpec((None, TQ, DIFF_DV), lambda b, h, i: (b, i, h)),
                  pl.BlockSpec((None, s, DIFF_DV), lambda b, h, i: (b, 0, k_col0 + h)),
                  pl.BlockSpec((None, s, DIFF_DV), lambda b, h, i: (b, 0, v_col0 + h)),
                  _resident((TQ, TQ)), _resident((TQ, LANES)),
                  _resident((4, DIFF_DQK)), _resident((1, DIFF_DV))],
        out_specs=pl.BlockSpec((None, TQ, DIFF_DV), lambda b, h, i: (b, i, h)),
        out_shape=jax.ShapeDtypeStruct((nb, s, D_INNER), BF16),
        scratch_shapes=[pltpu.VMEM((2, TQ, 2 * DIFF_DQK), BF16), pltpu.VMEM((2, TQ, 2 * DIFF_DQK), BF16),
                        pltpu.VMEM((2, TQ, 1), F32), pltpu.VMEM((2, TQ, 1), F32),
                        pltpu.VMEM((2, TQ, DIFF_DV), F32)],
        compiler_params=_cparams(("parallel", "parallel", "arbitrary")),
        name="diff_attention_online",
    )(sl, qkvz3, qkvz3, qkvz3, absd, kx, lam, og)


def _prep_a(g, w_in, v_g, w_s, b_s, w_out):
    return (g[None], w_in.astype(BF16), v_g[None], w_s.astype(BF16), b_s[:, :, None], w_out.astype(BF16))


def _prep_b(g, w_in, w_gate, gate_bias, o_g, w_out):
    c_q, c_k, c_v, c_g, c_a = 0, GLA_KEY, 2 * GLA_KEY, 2 * GLA_KEY + D_INNER, 2 * GLA_KEY + 2 * D_INNER
    w_main = jnp.concatenate([w_in[:, c_v:c_g], w_in[:, c_g:c_a], w_in[:, c_q:c_k], w_in[:, c_k:c_v]],
                             axis=1).astype(BF16)
    w_a = jnp.pad(w_in[:, c_a:], ((0, 0), (0, LANES - 2 * GLA_RANK))).astype(BF16)
    wg = jnp.zeros((LANES, 2 * GLA_KEY), F32)
    wg = wg.at[:GLA_RANK, :GLA_KEY].set(w_gate[0]).at[GLA_RANK:2 * GLA_RANK, GLA_KEY:].set(w_gate[1])
    rows = np.arange(MXU_DEPTH)
    same_chunk = (rows[:, None] // GLA_CHUNK) == (rows[None, :] // GLA_CHUNK)
    tril = jnp.asarray(same_chunk & (rows[None, :] <= rows[:, None]), BF16)
    triu = jnp.asarray(same_chunk & (rows[None, :] >= rows[:, None]), BF16)
    return (g[None], w_main, w_a, wg.astype(BF16), gate_bias.reshape(1, -1), tril, triu, o_g[None],
            w_out.astype(BF16))


def _layer_b(x, g, w_main, w_a, wg, gate_bias, tril, triu, o_g, w_out):
    nb = x.shape[0] // SEQ
    main, cf, cb = _inproj_b(x, g, w_main, w_a, wg, gate_bias, tril, triu)
    o_f, o_b = _gla(main.reshape(nb, SEQ, -1), cf.reshape(nb, SEQ, -1), cb.reshape(nb, SEQ, -1))
    return _outproj_b(o_f.reshape(-1, D_INNER), o_b.reshape(-1, D_INNER), main, o_g, w_out, x)


def _prep_c(g, w_in, q_g, k_g, lam, o_g, w_out, lambda_init):
    logit_bound = jnp.max(jnp.abs(q_g)) * jnp.max(jnp.abs(k_g)) * (DIFF_DQK ** 0.5 * LOG2E * 1.02)
    return (g[None], w_in.astype(BF16), q_g[None], k_g[None], lam, o_g[None], w_out.astype(BF16),
            logit_bound <= MAX_UNSHIFTED_LOG2, lambda_init)


def _layer_c(x, g, w_in, q_g, k_g, lam, o_g, w_out, logits_bounded, lambda_init):
    nb = x.shape[0] // SEQ
    qkvz = _inproj_c(x, g, w_in, q_g, k_g)
    sl_np = (2.0 ** (-8.0 * np.arange(1, DIFF_HEADS + 1) / DIFF_HEADS) * LOG2E).astype(np.float32)
    qkvz3 = qkvz.reshape(nb, SEQ, -1)
    o = lax.cond(logits_bounded,
                 lambda: _attention_bounded(qkvz3, sl_np, lam, o_g, lambda_init),
                 lambda: _attention_online(qkvz3, jnp.asarray(sl_np), lam, o_g, lambda_init))
    return _outproj_c(o.reshape(-1, D_INNER), qkvz, w_out, x)


def kernel(x_prompt, x_sample, norm_g, a_w_in, a_v_g, a_w_s, a_b_s, a_w_out, b_w_in, b_w_gate, b_gate_bias,
           b_o_g, b_w_out, c_w_in, c_q_g, c_k_g, c_lam, c_o_g, c_w_out):
    layers = []
    for i in range(DEPTH):
        kind, j = i % 3, i // 3
        if kind == 0:
            layers.append((_layer_a_call, _prep_a(norm_g[i], a_w_in[j], a_v_g[j], a_w_s[j], a_b_s[j], a_w_out[j])))
        elif kind == 1:
            layers.append((_layer_b, _prep_b(norm_g[i], b_w_in[j], b_w_gate[j], b_gate_bias[j], b_o_g[j],
                                             b_w_out[j])))
        else:
            lambda_init = 0.8 - 0.6 * math.exp(-0.3 * i)
            layers.append((_layer_c, _prep_c(norm_g[i], c_w_in[j], c_q_g[j], c_k_g[j], c_lam[j], c_o_g[j],
                                             c_w_out[j], lambda_init)))

    def trunk(x_group):
        x = x_group.reshape(-1, D_MODEL)
        for layer, params in layers:
            x = layer(x, *params)
        return x.reshape(x_group.shape)

    return trunk(x_prompt), trunk(x_sample)
```
